```python
import jax, jax.numpy as jnp
from jax import lax
import numpy as np

D_MODEL = 1024
BATCH = 16
SEQ = 2048
DEPTH = 4
DEC_BATCH = 128
DEC_SEQ = 8
PAST_LEN = 8192
PAGE_SIZE = 128

N_A_LAYERS = DEPTH // 2
N_B_LAYERS = DEPTH - N_A_LAYERS
N_DENSE = (DEPTH + 1) // 2
N_MOE = DEPTH // 2
CHUNK = 128
N_GROUPS = 8
D_GATE = 2 * D_MODEL
N_HEADS = 8
Q_RANK = 3 * D_MODEL // 8
KV_RANK = D_MODEL // 4
D_NOPE = 128
D_ROPE = 64
D_V = 128
QBLOCK = 128
ROPE_THETA = 10000.0
D_FF = 2816
N_EXPERTS = 8
TOP_K = 2
EPS = 1e-6
ATTN_SCALE = (D_NOPE + D_ROPE) ** -0.5

kernel_name = 'yoco_chunk_gmlp_mla_moe_step'


def rmsnorm(x, g):
    xf = x.astype(jnp.float32)
    y = xf * lax.rsqrt(jnp.mean(xf * xf, axis=-1, keepdims=True) + EPS)
    return (y * g.astype(jnp.float32)).astype(x.dtype)


def rope(x, pos):
    half = D_ROPE // 2
    inv = ROPE_THETA ** (-jnp.arange(half, dtype=jnp.float32) / half)
    ang = pos.astype(jnp.float32)[:, None] * inv[None, :]
    shape = (ang.shape[0],) + (1,) * (x.ndim - 3) + (half,)
    cos = jnp.cos(ang).reshape(shape)
    sin = jnp.sin(ang).reshape(shape)
    xf = x.astype(jnp.float32)
    x1, x2 = xf[..., :half], xf[..., half:]
    return jnp.concatenate([x1 * cos - x2 * sin, x1 * sin + x2 * cos], axis=-1).astype(x.dtype)


def chunk_gmlp(x, norm_g, w_in, b_in, sgu_g, w_s, b_s, w_out, b_out):
    bsz, s, _ = x.shape
    L = min(s, CHUNK)
    h = rmsnorm(x, norm_g)
    z = jax.nn.gelu(h @ w_in + b_in)
    u, v = z[..., :D_GATE], z[..., D_GATE:]
    v = rmsnorm(v, sgu_g)
    vg = v.reshape(bsz, s // L, L, N_GROUPS, D_GATE // N_GROUPS)
    w = w_s[:, :L, :L] * jnp.tril(jnp.ones((L, L), w_s.dtype))
    mix = jnp.einsum('gij,bnjgc->bnigc', w, vg) + b_s[:, :L].T[None, None, :, :, None]
    out = (u * mix.reshape(bsz, s, D_GATE)) @ w_out + b_out
    return out, v


def dense_swiglu(h, wg, wu, wd):
    return (jax.nn.silu(h @ wg) * (h @ wu)) @ wd


def moe_swiglu(h, w_router, wg, wu, wd):
    shp = h.shape
    t = h.reshape(-1, D_MODEL)
    probs = jax.nn.softmax((t @ w_router).astype(jnp.float32), axis=-1)
    top_p, top_i = lax.top_k(probs, TOP_K)
    top_p = top_p / jnp.sum(top_p, axis=-1, keepdims=True)
    gates = jnp.sum(jax.nn.one_hot(top_i, N_EXPERTS, dtype=jnp.float32) * top_p[..., None], axis=-2)
    gates = gates.astype(h.dtype)

    def expert(acc, xs):
        wg_e, wu_e, wd_e, g_e = xs
        y = (jax.nn.silu(t @ wg_e) * (t @ wu_e)) @ wd_e
        return acc + g_e[:, None] * y, None

    out, _ = lax.scan(expert, jnp.zeros_like(t), (wg, wu, wd, gates.T))
    return out.reshape(shp)


def channel_mixer(x, layer, p):
    h = rmsnorm(x, p['f_norm_g'][layer])
    i = layer // 2
    if layer % 2 == 0:
        return dense_swiglu(h, p['d_w_gate'][i], p['d_w_up'][i], p['d_w_down'][i])
    return moe_swiglu(h, p['m_w_router'][i], p['m_w_gate'][i], p['m_w_up'][i], p['m_w_down'][i])


def shared_key_side(x, pos, kv_norm_g, w_dkv, kv_lat_g, w_kr, kr_g):
    h = rmsnorm(x, kv_norm_g)
    c = rmsnorm(h @ w_dkv, kv_lat_g)
    kr = rope(rmsnorm(h @ w_kr, kr_g), pos)
    return c, kr


def key_rms_scale(c, w_uk):
    k = jnp.einsum('tr,rhd->thd', c, w_uk).astype(jnp.float32)
    return lax.rsqrt(jnp.mean(k * k, axis=-1) + EPS)


def mla_attend_block(q_abs, q_rope, qpos, c, kr, r, kpos):
    s = (jnp.einsum('bqhr,bkr->bhqk', q_abs, c).astype(jnp.float32) * jnp.swapaxes(r, 1, 2)[:, :, None, :]
         + jnp.einsum('bqhd,bkd->bhqk', q_rope, kr).astype(jnp.float32))
    s = jnp.where(kpos[None, :] <= qpos[:, None], s * ATTN_SCALE, -jnp.inf)
    prob = jax.nn.softmax(s, axis=-1).astype(c.dtype)
    return jnp.einsum('bhqk,bkr->bqhr', prob, c)


def mla_attend(q_abs, q_rope, qpos, c, kr, r, kpos):
    bsz, q_len = q_abs.shape[:2]
    L = min(q_len, QBLOCK)
    nb = q_len // L

    def split(a):
        return jnp.swapaxes(a.reshape((bsz, nb, L) + a.shape[2:]), 0, 1)

    o = lax.map(lambda blk: mla_attend_block(blk[0], blk[1], blk[2], c, kr, r, kpos),
                (split(q_abs), split(q_rope), qpos.reshape(nb, L)))
    return jnp.swapaxes(o, 0, 1).reshape(bsz, q_len, N_HEADS, KV_RANK)


def mla_mixer(x, pos, c, kr, r, kpos, norm_g, w_dq, q_lat_g, w_uq, qn_g, qr_g, w_o, kn_g, w_uk, w_uv):
    bsz, s, _ = x.shape
    h = rmsnorm(x, norm_g)
    cq = rmsnorm(h @ w_dq, q_lat_g)
    q = jnp.einsum('bsr,rhd->bshd', cq, w_uq)
    qn = rmsnorm(q[..., :D_NOPE], qn_g)
    qr = rope(rmsnorm(q[..., D_NOPE:], qr_g), pos)
    q_abs = jnp.einsum('bshd,rhd->bshr', qn * kn_g, w_uk)
    o_lat = mla_attend(q_abs, qr, pos, c, kr, r, kpos)
    o = jnp.einsum('bshr,rhd->bshd', o_lat, w_uv)
    return o.reshape(bsz, s, N_HEADS * D_V) @ w_o


def trunk(x, pos, past_c, past_kr, p):
    v_rows = []
    c_new = kr_new = c_all = kr_all = r_all = kpos = None
    for layer in range(DEPTH):
        if layer < N_A_LAYERS:
            a = layer
            out, v = chunk_gmlp(x, p['a_norm_g'][a], p['a_w_in'][a], p['a_b_in'][a], p['a_sgu_g'][a],
                                p['a_w_s'][a], p['a_b_s'][a], p['a_w_out'][a], p['a_b_out'][a])
            x = x + out
            v_rows.append(v)
        else:
            if layer == N_A_LAYERS:
                c_new, kr_new = shared_key_side(x, pos, p['kv_norm_g'], p['w_dkv'], p['kv_lat_g'],
                                                p['w_kr'], p['kr_g'])
                if past_c is None:
                    c_all, kr_all, kpos = c_new, kr_new, pos
                else:
                    c_all = jnp.concatenate([past_c, c_new], axis=1)
                    kr_all = jnp.concatenate([past_kr, kr_new], axis=1)
                    kpos = jnp.arange(c_all.shape[1], dtype=jnp.int32)
                w_uk = p['w_uk']
                r_all = lax.map(lambda cb: key_rms_scale(cb, w_uk), c_all)
            b = layer - N_A_LAYERS
            x = x + mla_mixer(x, pos, c_all, kr_all, r_all, kpos, p['b_norm_g'][b], p['b_w_dq'][b],
                              p['b_q_lat_g'][b], p['b_w_uq'][b], p['b_qn_g'][b], p['b_qr_g'][b],
                              p['b_w_o'][b], p['kn_g'], p['w_uk'], p['w_uv'])
        x = x + channel_mixer(x, layer, p)
    return x, c_new, kr_new, jnp.stack(v_rows)


def setup_inputs(seed: int = 0) -> dict:
    key = jax.random.key(seed)
    ks = jax.random.split(key, 40)
    f32 = jnp.float32

    def nrm(k, shape, scale):
        return scale * jax.random.normal(k, shape, f32)

    def gain(k, shape):
        return 1.0 + 0.05 * jax.random.normal(k, shape, f32)

    n_pages = PAST_LEN // PAGE_SIZE
    n_pool = (5 * DEC_BATCH * n_pages) // 4
    perm = jax.random.permutation(ks[4], n_pool)[: DEC_BATCH * n_pages]
    page_table = perm.reshape(DEC_BATCH, n_pages).astype(jnp.int32)
    na, nb = N_A_LAYERS, N_B_LAYERS
    return {
        'x_prompt': nrm(ks[0], (BATCH, SEQ, D_MODEL), 1.0),
        'x_sample': nrm(ks[1], (DEC_BATCH, DEC_SEQ, D_MODEL), 1.0),
        'cache_latent': nrm(ks[2], (n_pool, PAGE_SIZE, KV_RANK), 1.0),
        'cache_krope': nrm(ks[3], (n_pool, PAGE_SIZE, D_ROPE), 1.0),
        'page_table': page_table,
        'a_norm_g': gain(ks[5], (na, D_MODEL)),
        'a_w_in': nrm(ks[6], (na, D_MODEL, 2 * D_GATE), D_MODEL ** -0.5),
        'a_b_in': nrm(ks[7], (na, 2 * D_GATE), 0.02),
        'a_sgu_g': gain(ks[8], (na, D_GATE)),
        'a_w_s': nrm(ks[9], (na, N_GROUPS, CHUNK, CHUNK), CHUNK ** -0.5),
        'a_b_s': 1.0 + nrm(ks[10], (na, N_GROUPS, CHUNK), 0.02),
        'a_w_out': nrm(ks[11], (na, D_GATE, D_MODEL), D_GATE ** -0.5),
        'a_b_out': nrm(ks[12], (na, D_MODEL), 0.02),
        'kv_norm_g': gain(ks[13], (D_MODEL,)),
        'w_dkv': nrm(ks[14], (D_MODEL, KV_RANK), D_MODEL ** -0.5),
        'kv_lat_g': gain(ks[15], (KV_RANK,)),
        'w_kr': nrm(ks[16], (D_MODEL, D_ROPE), D_MODEL ** -0.5),
        'kr_g': gain(ks[17], (D_ROPE,)),
        'kn_g': gain(ks[18], (D_NOPE,)),
        'w_uk': nrm(ks[19], (KV_RANK, N_HEADS, D_NOPE), KV_RANK ** -0.5),
        'w_uv': nrm(ks[20], (KV_RANK, N_HEADS, D_V), KV_RANK ** -0.5),
        'b_norm_g': gain(ks[21], (nb, D_MODEL)),
        'b_w_dq': nrm(ks[22], (nb, D_MODEL, Q_RANK), D_MODEL ** -0.5),
        'b_q_lat_g': gain(ks[23], (nb, Q_RANK)),
        'b_w_uq': nrm(ks[24], (nb, Q_RANK, N_HEADS, D_NOPE + D_ROPE), Q_RANK ** -0.5),
        'b_qn_g': gain(ks[25], (nb, D_NOPE)),
        'b_qr_g': gain(ks[26], (nb, D_ROPE)),
        'b_w_o': nrm(ks[27], (nb, N_HEADS * D_V, D_MODEL), (N_HEADS * D_V) ** -0.5),
        'f_norm_g': gain(ks[28], (DEPTH, D_MODEL)),
        'd_w_gate': nrm(ks[29], (N_DENSE, D_MODEL, D_FF), D_MODEL ** -0.5),
        'd_w_up': nrm(ks[30], (N_DENSE, D_MODEL, D_FF), D_MODEL ** -0.5),
        'd_w_down': nrm(ks[31], (N_DENSE, D_FF, D_MODEL), D_FF ** -0.5),
        'm_w_router': nrm(ks[32], (N_MOE, D_MODEL, N_EXPERTS), D_MODEL ** -0.5),
        'm_w_gate': nrm(ks[33], (N_MOE, N_EXPERTS, D_MODEL, D_FF), D_MODEL ** -0.5),
        'm_w_up': nrm(ks[34], (N_MOE, N_EXPERTS, D_MODEL, D_FF), D_MODEL ** -0.5),
        'm_w_down': nrm(ks[35], (N_MOE, N_EXPERTS, D_FF, D_MODEL), D_FF ** -0.5),
    }


def reference(x_prompt, x_sample, cache_latent, cache_krope, page_table,
              a_norm_g, a_w_in, a_b_in, a_sgu_g, a_w_s, a_b_s, a_w_out, a_b_out,
              kv_norm_g, w_dkv, kv_lat_g, w_kr, kr_g, kn_g, w_uk, w_uv,
              b_norm_g, b_w_dq, b_q_lat_g, b_w_uq, b_qn_g, b_qr_g, b_w_o,
              f_norm_g, d_w_gate, d_w_up, d_w_down, m_w_router, m_w_gate, m_w_up, m_w_down):
    p = {
        'a_norm_g': a_norm_g, 'a_w_in': a_w_in, 'a_b_in': a_b_in, 'a_sgu_g': a_sgu_g,
        'a_w_s': a_w_s, 'a_b_s': a_b_s, 'a_w_out': a_w_out, 'a_b_out': a_b_out,
        'kv_norm_g': kv_norm_g, 'w_dkv': w_dkv, 'kv_lat_g': kv_lat_g, 'w_kr': w_kr, 'kr_g': kr_g,
        'kn_g': kn_g, 'w_uk': w_uk, 'w_uv': w_uv,
        'b_norm_g': b_norm_g, 'b_w_dq': b_w_dq, 'b_q_lat_g': b_q_lat_g, 'b_w_uq': b_w_uq,
        'b_qn_g': b_qn_g, 'b_qr_g': b_qr_g, 'b_w_o': b_w_o,
        'f_norm_g': f_norm_g, 'd_w_gate': d_w_gate, 'd_w_up': d_w_up, 'd_w_down': d_w_down,
        'm_w_router': m_w_router, 'm_w_gate': m_w_gate, 'm_w_up': m_w_up, 'm_w_down': m_w_down,
    }
    pos_p = jnp.arange(x_prompt.shape[1], dtype=jnp.int32)
    y_p, lat_p, kr_p, _ = trunk(x_prompt, pos_p, None, None, p)
    n_seq, n_pages = page_table.shape
    past_len = n_pages * cache_latent.shape[1]
    past_c = cache_latent[page_table].reshape(n_seq, past_len, KV_RANK)
    past_kr = cache_krope[page_table].reshape(n_seq, past_len, D_ROPE)
    pos_s = past_len + jnp.arange(x_sample.shape[1], dtype=jnp.int32)
    y_s, lat_s, kr_s, v_s = trunk(x_sample, pos_s, past_c, past_kr, p)
    return (y_p, y_s, lat_p, kr_p, lat_s, kr_s, v_s)
```

```python
import functools

import jax
import jax.numpy as jnp
from jax import lax
from jax.experimental import pallas as pl
from jax.experimental.pallas import tpu as pltpu

EPS = 1e-6
ROPE_THETA = 10000.0
CHUNK = 128
N_GROUPS = 8
N_HEADS = 8
D_NOPE = 128
D_ROPE = 64
D_V = 128
TOP_K = 2
QBLOCK = 128
LANES = 128
VMEM_LIMIT_BYTES = 60 * 1024 * 1024

BF16 = jnp.bfloat16
F32 = jnp.float32
NT_DIMS = (((1,), (1,)), ((), ()))


def _dot(a, b):
    return jnp.dot(a, b, preferred_element_type=F32)


def _dot_nt(a, b):
    return lax.dot_general(a, b, NT_DIMS, preferred_element_type=F32)


def _rms(x, g):
    ms = jnp.mean(x * x, axis=-1, keepdims=True)
    return x * lax.rsqrt(ms + EPS) * g


def _rope_pairs(a, g2, cs):
    ms = jnp.sum(a * a, axis=-1, keepdims=True) * (1.0 / LANES)
    t = a * lax.rsqrt(ms + EPS) * g2 * cs
    return t + pltpu.roll(t, D_ROPE, 1)


def _resident(shape):
    nd = len(shape)
    return pl.BlockSpec(shape, lambda *_: (0,) * nd, pipeline_mode=pl.Buffered(1))


def _params(sem):
    return pltpu.CompilerParams(dimension_semantics=sem, vmem_limit_bytes=VMEM_LIMIT_BYTES)


def _gmlp_kernel(n_p_tiles, x_ref, ng_ref, win_ref, bin_ref, sg_ref, ws_ref, bs_ref, wout_ref, bout_ref,
                 xo_ref, v_ref, zu_ref, vb_ref, gated_ref):
    i = pl.program_id(0)
    x = x_ref[...]
    h = _rms(x, ng_ref[...]).astype(BF16)
    dg = sg_ref.shape[1]
    zu_ref[...] = jax.nn.gelu(_dot(h, win_ref[:, :dg]) + bin_ref[:, :dg])
    v = _rms(jax.nn.gelu(_dot(h, win_ref[:, dg:]) + bin_ref[:, dg:]), sg_ref[...])
    vb_ref[...] = v.astype(BF16)

    @pl.when(i >= n_p_tiles)
    def _():
        v_ref[...] = v

    gw = dg // N_GROUPS
    for c in range(x.shape[0] // CHUNK):
        rows = slice(c * CHUNK, (c + 1) * CHUNK)
        for g in range(N_GROUPS):
            cols = slice(g * gw, (g + 1) * gw)
            mix = _dot(ws_ref[0, g], vb_ref[rows, cols]) + bs_ref[0, :, cols]
            gated_ref[rows, cols] = (zu_ref[rows, cols] * mix).astype(BF16)
    xo_ref[...] = x + _dot(gated_ref[...], wout_ref[...]) + bout_ref[...]


def _gmlp_layer(x, n_p_tiles, tm, ng, w_in, b_in, sg, ws_eff, bs_eff, w_out, b_out):
    t, d = x.shape
    dg = sg.shape[1]
    n_tiles = t // tm
    t_s = t - n_p_tiles * tm
    sel = lambda i: jnp.where(i >= n_p_tiles, 1, 0)
    return pl.pallas_call(
        functools.partial(_gmlp_kernel, n_p_tiles),
        grid=(n_tiles,),
        in_specs=[
            pl.BlockSpec((tm, d), lambda i: (i, 0)),
            _resident(ng.shape), _resident(w_in.shape), _resident(b_in.shape), _resident(sg.shape),
            pl.BlockSpec((1,) + ws_eff.shape[1:], lambda i: (sel(i), 0, 0, 0)),
            pl.BlockSpec((1,) + bs_eff.shape[1:], lambda i: (sel(i), 0, 0)),
            _resident(w_out.shape), _resident(b_out.shape),
        ],
        out_specs=[
            pl.BlockSpec((tm, d), lambda i: (i, 0)),
            pl.BlockSpec((tm, dg), lambda i: (jnp.maximum(i - n_p_tiles, 0), 0)),
        ],
        out_shape=[jax.ShapeDtypeStruct((t, d), F32), jax.ShapeDtypeStruct((t_s, dg), F32)],
        scratch_shapes=[pltpu.VMEM((tm, dg), F32), pltpu.VMEM((tm, dg), BF16), pltpu.VMEM((tm, dg), BF16)],
        compiler_params=_params(("arbitrary",)),
        name="gmlp_layer",
    )(x, ng, w_in, b_in, sg, ws_eff, bs_eff, w_out, b_out)


def _ffn_kernel(x_ref, ng_ref, wg_ref, wu_ref, wd_ref, xo_ref):
    x = x_ref[...]
    h = _rms(x, ng_ref[...]).astype(BF16)
    t = (jax.nn.silu(_dot(h, wg_ref[...])) * _dot(h, wu_ref[...])).astype(BF16)
    xo_ref[...] = x + _dot(t, wd_ref[...])


def _ffn_dense(x, tm, ng, wg, wu, wd):
    t, d = x.shape
    return pl.pallas_call(
        _ffn_kernel,
        grid=(t // tm,),
        in_specs=[pl.BlockSpec((tm, d), lambda i: (i, 0)),
                  _resident(ng.shape), _resident(wg.shape), _resident(wu.shape), _resident(wd.shape)],
        out_specs=pl.BlockSpec((tm, d), lambda i: (i, 0)),
        out_shape=jax.ShapeDtypeStruct((t, d), F32),
        compiler_params=_params(("arbitrary",)),
        name="ffn_dense",
    )(x, ng, wg, wu, wd)


def _top2_gates(probs):
    ne = probs.shape[1]
    lane = lax.broadcasted_iota(jnp.int32, probs.shape, 1)
    m1 = jnp.max(probs, axis=1, keepdims=True)
    i1 = jnp.min(jnp.where(probs == m1, lane, ne), axis=1, keepdims=True)
    rest = jnp.where(lane == i1, -1.0, probs)
    m2 = jnp.max(rest, axis=1, keepdims=True)
    i2 = jnp.min(jnp.where(rest == m2, lane, ne), axis=1, keepdims=True)
    den = m1 + m2
    return jnp.where(lane == i1, m1 / den, 0.0) + jnp.where(lane == i2, m2 / den, 0.0)


def _moe_kernel(x_ref, ng_ref, wr_ref, wg_ref, wu_ref, wd_ref, xo_ref, h_ref, gates_ref, acc_ref):
    e = pl.program_id(1)
    f = pl.program_id(2)

    @pl.when((e == 0) & (f == 0))
    def _():
        hb = _rms(x_ref[...], ng_ref[...]).astype(BF16)
        h_ref[...] = hb
        logits = _dot(hb, wr_ref[...])
        ex = jnp.exp(logits - jnp.max(logits, axis=1, keepdims=True))
        gates_ref[...] = _top2_gates(ex / jnp.sum(ex, axis=1, keepdims=True))
        acc_ref[...] = jnp.zeros_like(acc_ref)

    hb = h_ref[...]
    t = (jax.nn.silu(_dot(hb, wg_ref[0])) * _dot(hb, wu_ref[0])).astype(BF16)
    lane = lax.broadcasted_iota(jnp.int32, gates_ref.shape, 1)
    ge = jnp.sum(jnp.where(lane == e, gates_ref[...], 0.0), axis=1, keepdims=True)
    acc_ref[...] += ge * _dot(t, wd_ref[0])

    @pl.when((e == pl.num_programs(1) - 1) & (f == pl.num_programs(2) - 1))
    def _():
        xo_ref[...] = x_ref[...] + acc_ref[...]


def _moe_layer(x, tm, ng, w_router, wg, wu, wd, ff_split):
    t, d = x.shape
    ne, _, dff = wg.shape
    fb = dff // ff_split
    return pl.pallas_call(
        _moe_kernel,
        grid=(t // tm, ne, ff_split),
        in_specs=[pl.BlockSpec((tm, d), lambda i, e, f: (i, 0)),
                  _resident(ng.shape), _resident(w_router.shape),
                  pl.BlockSpec((1, d, fb), lambda i, e, f: (e, 0, f)),
                  pl.BlockSpec((1, d, fb), lambda i, e, f: (e, 0, f)),
                  pl.BlockSpec((1, fb, d), lambda i, e, f: (e, f, 0))],
        out_specs=pl.BlockSpec((tm, d), lambda i, e, f: (i, 0)),
        out_shape=jax.ShapeDtypeStruct((t, d), F32),
        scratch_shapes=[pltpu.VMEM((tm, d), BF16), pltpu.VMEM((tm, ne), F32), pltpu.VMEM((tm, d), F32)],
        compiler_params=_params(("arbitrary", "arbitrary", "arbitrary")),
        name="moe_layer",
    )(x, ng, w_router, wg, wu, wd)


def _keyside_kernel(x_ref, ng_ref, wdkv_ref, lg_ref, wkr_ref, krg_ref, cs_ref, wukt_ref,
                    c_ref, kr_ref, cb_ref, krp_ref, rt_ref):
    h = _rms(x_ref[...], ng_ref[...]).astype(BF16)
    c = _rms(_dot(h, wdkv_ref[...]), lg_ref[...])
    cb = c.astype(BF16)
    c_ref[...] = c
    cb_ref[...] = cb
    full = _rope_pairs(_dot(h, wkr_ref[...]), krg_ref[...], cs_ref[...])
    kr_ref[...] = full[:, :D_ROPE]
    lane = lax.broadcasted_iota(jnp.int32, full.shape, 1)
    krp_ref[...] = jnp.where(lane < D_ROPE, full, 0.0).astype(BF16)
    rt_ref[...] = _key_rms_scale_t(wukt_ref[...], cb)


def _key_rms_scale_t(wukt, cb):
    kt = _dot_nt(wukt, cb)
    ssq = jnp.sum((kt * kt).reshape(N_HEADS, D_NOPE, kt.shape[1]), axis=1)
    return lax.rsqrt(ssq * (1.0 / D_NOPE) + EPS)


def _key_side(x, tm, cs_map, ng, w_dkv, lg, w_kr2, krg2, cs_tab, wukt):
    t, d = x.shape
    r = w_dkv.shape[1]
    row = lambda i: (i, 0)
    return pl.pallas_call(
        _keyside_kernel,
        grid=(t // tm,),
        in_specs=[pl.BlockSpec((tm, d), row),
                  _resident(ng.shape), _resident(w_dkv.shape), _resident(lg.shape),
                  _resident(w_kr2.shape), _resident(krg2.shape),
                  pl.BlockSpec((tm, LANES), lambda i: (cs_map(i), 0)),
                  _resident(wukt.shape)],
        out_specs=[pl.BlockSpec((tm, r), row), pl.BlockSpec((tm, D_ROPE), row), pl.BlockSpec((tm, r), row),
                   pl.BlockSpec((tm, LANES), row), pl.BlockSpec((N_HEADS, tm), lambda i: (0, i))],
        out_shape=[jax.ShapeDtypeStruct((t, r), F32), jax.ShapeDtypeStruct((t, D_ROPE), F32),
                   jax.ShapeDtypeStruct((t, r), BF16), jax.ShapeDtypeStruct((t, LANES), BF16),
                   jax.ShapeDtypeStruct((N_HEADS, t), F32)],
        compiler_params=_params(("arbitrary",)),
        name="key_side",
    )(x, ng, w_dkv, lg, w_kr2, krg2, cs_tab, wukt)


def _past_scale_kernel(n_pg, pt_ref, *refs):
    wukt_ref = refs[0]
    pages = refs[1:1 + n_pg]
    rt_ref = refs[1 + n_pg]
    cblk_ref = refs[2 + n_pg]
    for g in range(n_pg):
        cblk_ref[g * CHUNK:(g + 1) * CHUNK, :] = pages[g][0].astype(BF16)
    rt_ref[0] = _key_rms_scale_t(wukt_ref[...], cblk_ref[...])


def _page_specs(cache, n_pages, n_pg):
    page, width = cache.shape[1:]
    return [pl.BlockSpec((1, page, width),
                         functools.partial(lambda g, b, j, pt: (pt[b * n_pages + j * n_pg + g], 0, 0), g))
            for g in range(n_pg)]


def _past_scale(page_table, cache_latent, wukt, n_pg):
    n_seq, n_pages = page_table.shape
    page, r = cache_latent.shape[1:]
    gk = n_pg * page
    return pl.pallas_call(
        functools.partial(_past_scale_kernel, n_pg),
        grid_spec=pltpu.PrefetchScalarGridSpec(
            num_scalar_prefetch=1,
            grid=(n_seq, n_pages // n_pg),
            in_specs=[pl.BlockSpec(wukt.shape, lambda b, j, pt: (0, 0), pipeline_mode=pl.Buffered(1))]
            + _page_specs(cache_latent, n_pages, n_pg),
            out_specs=pl.BlockSpec((1, N_HEADS, gk), lambda b, j, pt: (b, 0, j)),
            scratch_shapes=[pltpu.VMEM((gk, r), BF16)]),
        out_shape=jax.ShapeDtypeStruct((n_seq, N_HEADS, n_pages * page), F32),
        compiler_params=_params(("arbitrary", "arbitrary")),
        name="past_key_scale",
    )(page_table.reshape(-1), wukt, *([cache_latent] * n_pg))


def _qside_kernel(x_ref, ng_ref, wdq_ref, qlg_ref, wn_ref, wr_ref, qng_ref, kng_ref, qrg_ref, cs_ref, wukt_ref,
                  qa_ref, qr_ref):
    h = _rms(x_ref[...], ng_ref[...]).astype(BF16)
    cq = _rms(_dot(h, wdq_ref[...]), qlg_ref[...]).astype(BF16)
    qn_all = _dot(cq, wn_ref[...])
    qr_all = _dot(cq, wr_ref[...])
    cs = cs_ref[...]
    for hh in range(N_HEADS):
        cols = slice(hh * LANES, (hh + 1) * LANES)
        qn = (_rms(qn_all[:, cols], qng_ref[...]) * kng_ref[...]).astype(BF16)
        qa_ref[hh] = _dot(qn, wukt_ref[hh]).astype(BF16)
        qr_ref[hh] = _rope_pairs(qr_all[:, cols], qrg_ref[...], cs).astype(BF16)


def _q_side(x, tm, cs_map, ng, w_dq, qlg, wn, wr2, qng, kng, qrg2, cs_tab, wukt3):
    t, d = x.shape
    r = wukt3.shape[2]
    return pl.pallas_call(
        _qside_kernel,
        grid=(t // tm,),
        in_specs=[pl.BlockSpec((tm, d), lambda i: (i, 0)),
                  _resident(ng.shape), _resident(w_dq.shape), _resident(qlg.shape), _resident(wn.shape),
                  _resident(wr2.shape), _resident(qng.shape), _resident(kng.shape), _resident(qrg2.shape),
                  pl.BlockSpec((tm, LANES), lambda i: (cs_map(i), 0)),
                  _resident(wukt3.shape)],
        out_specs=[pl.BlockSpec((N_HEADS, tm, r), lambda i: (0, i, 0)),
                   pl.BlockSpec((N_HEADS, tm, LANES), lambda i: (0, i, 0))],
        out_shape=[jax.ShapeDtypeStruct((N_HEADS, t, r), BF16), jax.ShapeDtypeStruct((N_HEADS, t, LANES), BF16)],
        compiler_params=_params(("arbitrary",)),
        name="q_side",
    )(x, ng, w_dq, qlg, wn, wr2, qng, kng, qrg2, cs_tab, wukt3)


def _softmax_step(s, cb, m, l, acc):
    m_new = jnp.maximum(m, jnp.max(s, axis=1, keepdims=True))
    alpha = jnp.exp(m - m_new)
    p = jnp.exp(s - m_new)
    l_new = alpha * l + jnp.sum(p, axis=1, keepdims=True)
    acc_new = alpha * acc + _dot(p.astype(BF16), cb)
    return m_new, l_new, acc_new


def _scores(q, qr, cb, krb, rt, scale):
    nq = q.shape[0] // N_HEADS
    nk = cb.shape[0]
    s1 = _dot_nt(q, cb).reshape(N_HEADS, nq, nk) * rt[:, None, :]
    return (s1.reshape(N_HEADS * nq, nk) + _dot_nt(qr, krb)) * scale


def _prompt_attn_kernel(bq, bk, scale, qa_ref, qr_ref, cb_ref, krp_ref, rt_ref, o_ref):
    i = pl.program_id(1)
    r = qa_ref.shape[2]
    q = qa_ref[...].reshape(N_HEADS * bq, r)
    qr = qr_ref[...].reshape(N_HEADS * bq, LANES)
    qpos = i * bq + lax.broadcasted_iota(jnp.int32, (N_HEADS, bq, bk), 1).reshape(N_HEADS * bq, bk)
    kidx = lax.broadcasted_iota(jnp.int32, (N_HEADS * bq, bk), 1)

    def body(j, carry):
        k0 = pl.multiple_of(j * bk, bk)
        cb = cb_ref[pl.ds(k0, bk), :]
        s = _scores(q, qr, cb, krp_ref[pl.ds(k0, bk), :], rt_ref[:, pl.ds(k0, bk)], scale)
        s = jnp.where(kidx + k0 <= qpos, s, -jnp.inf)
        return _softmax_step(s, cb, *carry)

    init = (jnp.full((N_HEADS * bq, 1), -jnp.inf, F32), jnp.zeros((N_HEADS * bq, 1), F32),
            jnp.zeros((N_HEADS * bq, r), F32))
    _, l, acc = lax.fori_loop(0, (i * bq + bq + bk - 1) // bk, body, init)
    o_ref[...] = (acc / l).reshape(N_HEADS, bq, r).astype(BF16)


def _prompt_attention(qa, qr, cb, krp, rt, bsz, seq, bq, bk, scale):
    r = cb.shape[1]
    nq = seq // bq
    qmap = lambda b, i: (0, b * nq + i, 0)
    return pl.pallas_call(
        functools.partial(_prompt_attn_kernel, bq, bk, scale),
        grid=(bsz, nq),
        in_specs=[pl.BlockSpec((N_HEADS, bq, r), qmap), pl.BlockSpec((N_HEADS, bq, LANES), qmap),
                  pl.BlockSpec((seq, r), lambda b, i: (b, 0)), pl.BlockSpec((seq, LANES), lambda b, i: (b, 0)),
                  pl.BlockSpec((N_HEADS, seq), lambda b, i: (0, b))],
        out_specs=pl.BlockSpec((N_HEADS, bq, r), qmap),
        out_shape=jax.ShapeDtypeStruct((N_HEADS, bsz * seq, r), BF16),
        compiler_params=_params(("arbitrary", "arbitrary")),
        name="prompt_attention",
    )(qa, qr, cb, krp, rt)


def _sample_attn_kernel(n_pg, n_new, scale, pt_ref, *refs):
    qa_ref, qr_ref, rtp_ref, cn_ref, krn_ref, rtn_ref = refs[:6]
    cpages = refs[6:6 + n_pg]
    kpages = refs[6 + n_pg:6 + 2 * n_pg]
    o_ref, m_ref, l_ref, acc_ref, cblk_ref, kblk_ref = refs[6 + 2 * n_pg:]
    j = pl.program_id(1)

    @pl.when(j == 0)
    def _():
        m_ref[...] = jnp.full_like(m_ref, -jnp.inf)
        l_ref[...] = jnp.zeros_like(l_ref)
        acc_ref[...] = jnp.zeros_like(acc_ref)

    for g in range(n_pg):
        cblk_ref[g * CHUNK:(g + 1) * CHUNK, :] = cpages[g][0].astype(BF16)
        kblk_ref[g * CHUNK:(g + 1) * CHUNK, :] = kpages[g][0].astype(BF16)
    q = qa_ref[0]
    qr = qr_ref[0][:, :D_ROPE]
    cb = cblk_ref[...]
    s = _scores(q, qr, cb, kblk_ref[...], rtp_ref[0], scale)
    m, l, acc = _softmax_step(s, cb, m_ref[...], l_ref[...], acc_ref[...])
    m_ref[...] = m
    l_ref[...] = l
    acc_ref[...] = acc

    @pl.when(j == pl.num_programs(1) - 1)
    def _():
        pad = CHUNK - n_new
        cn = jnp.concatenate([cn_ref[0], jnp.zeros((pad, cn_ref.shape[2]), F32)], axis=0).astype(BF16)
        krn = jnp.concatenate([krn_ref[0], jnp.zeros((pad, D_ROPE), F32)], axis=0).astype(BF16)
        sn = _scores(q, qr, cn, krn, rtn_ref[0], scale)
        nrow = q.shape[0]
        qi = lax.broadcasted_iota(jnp.int32, (N_HEADS, nrow // N_HEADS, CHUNK), 1).reshape(nrow, CHUNK)
        ki = lax.broadcasted_iota(jnp.int32, (nrow, CHUNK), 1)
        sn = jnp.where(ki <= qi, sn, -jnp.inf)
        _, l2, acc2 = _softmax_step(sn, cn, m, l, acc)
        o_ref[0] = (acc2 / l2).astype(BF16)


def _sample_attention(page_table, qa_s, qr_s, rt_past, c_new, kr_new, rt_new, cache_latent, cache_krope,
                      n_pg, scale):
    n_seq, n_pages = page_table.shape
    page, r = cache_latent.shape[1:]
    nrow = qa_s.shape[1]
    n_new = c_new.shape[1]
    gk = n_pg * page
    seq_blk = lambda shape: pl.BlockSpec((1,) + shape, lambda b, j, pt: (b, 0, 0))
    return pl.pallas_call(
        functools.partial(_sample_attn_kernel, n_pg, n_new, scale),
        grid_spec=pltpu.PrefetchScalarGridSpec(
            num_scalar_prefetch=1,
            grid=(n_seq, n_pages // n_pg),
            in_specs=[seq_blk((nrow, r)), seq_blk((nrow, LANES)),
                      pl.BlockSpec((1, N_HEADS, gk), lambda b, j, pt: (b, 0, j)),
                      seq_blk((n_new, r)), seq_blk((n_new, D_ROPE)), seq_blk((N_HEADS, LANES))]
            + _page_specs(cache_latent, n_pages, n_pg) + _page_specs(cache_krope, n_pages, n_pg),
            out_specs=seq_blk((nrow, r)),
            scratch_shapes=[pltpu.VMEM((nrow, 1), F32), pltpu.VMEM((nrow, 1), F32), pltpu.VMEM((nrow, r), F32),
                            pltpu.VMEM((gk, r), BF16), pltpu.VMEM((gk, D_ROPE), BF16)]),
        out_shape=jax.ShapeDtypeStruct((n_seq, nrow, r), BF16),
        compiler_params=_params(("arbitrary", "arbitrary")),
        name="sample_attention",
    )(page_table.reshape(-1), qa_s, qr_s, rt_past, c_new, kr_new, rt_new,
      *([cache_latent] * n_pg), *([cache_krope] * n_pg))


def _attn_out_kernel(n_p_tiles, x_ref, op_ref, os_ref, wuv_ref, wo_ref, xo_ref):
    i = pl.program_id(0)

    def project(o_ref):
        o = jnp.concatenate([_dot(o_ref[hh], wuv_ref[hh]) for hh in range(N_HEADS)], axis=1).astype(BF16)
        xo_ref[...] = x_ref[...] + _dot(o, wo_ref[...])

    @pl.when(i < n_p_tiles)
    def _():
        project(op_ref)

    @pl.when(i >= n_p_tiles)
    def _():
        project(os_ref)


def _attn_out(x, o_p, o_s, n_p_tiles, tm, w_uv3, w_o):
    t, d = x.shape
    r = o_p.shape[2]
    return pl.pallas_call(
        functools.partial(_attn_out_kernel, n_p_tiles),
        grid=(t // tm,),
        in_specs=[pl.BlockSpec((tm, d), lambda i: (i, 0)),
                  pl.BlockSpec((N_HEADS, tm, r), lambda i: (0, jnp.minimum(i, n_p_tiles - 1), 0)),
                  pl.BlockSpec((N_HEADS, tm, r), lambda i: (0, jnp.maximum(i - n_p_tiles, 0), 0)),
                  _resident(w_uv3.shape), _resident(w_o.shape)],
        out_specs=pl.BlockSpec((tm, d), lambda i: (i, 0)),
        out_shape=jax.ShapeDtypeStruct((t, d), F32),
        compiler_params=_params(("arbitrary",)),
        name="attn_out",
    )(x, o_p, o_s, w_uv3, w_o)


def _swap_halves(a):
    half = a.shape[-1] // 2
    return jnp.concatenate([a[..., half:], a[..., :half]], axis=-1)


def _rope_table(pos):
    half = D_ROPE // 2
    inv = ROPE_THETA ** (-jnp.arange(half, dtype=F32) / half)
    ang = pos.astype(F32)[:, None] * inv[None, :]
    cos, sin = jnp.cos(ang), jnp.sin(ang)
    return jnp.concatenate([cos, cos, -sin, sin], axis=1)


def _spatial_weights(w_s, b_s, length, gw):
    reps = CHUNK // length
    w = w_s[:, :length, :length] * jnp.tril(jnp.ones((length, length), w_s.dtype))
    w_eff = jnp.einsum("st,gpq->gsptq", jnp.eye(reps, dtype=w.dtype), w).reshape(N_GROUPS, CHUNK, CHUNK)
    b_eff = jnp.repeat(jnp.tile(b_s[:, :length].T, (reps, 1)), gw, axis=1)
    return w_eff, b_eff


def kernel(x_prompt, x_sample, cache_latent, cache_krope, page_table, a_norm_g, a_w_in, a_b_in, a_sgu_g, a_w_s, a_b_s, a_w_out, a_b_out, kv_norm_g, w_dkv, kv_lat_g, w_kr, kr_g, kn_g, w_uk, w_uv, b_norm_g, b_w_dq, b_q_lat_g, b_w_uq, b_qn_g, b_qr_g, b_w_o, f_norm_g, d_w_gate, d_w_up, d_w_down, m_w_router, m_w_gate, m_w_up, m_w_down):
    bsz, seq, d = x_prompt.shape
    n_seq, dec = x_sample.shape[:2]
    t_p, t_s = bsz * seq, n_seq * dec
    n_pages = page_table.shape[1]
    past_len = n_pages * cache_latent.shape[1]
    depth = f_norm_g.shape[0]
    n_a = a_w_in.shape[0]
    dg = a_sgu_g.shape[1]
    kv_rank = w_dkv.shape[1]
    scale = float((D_NOPE + D_ROPE) ** -0.5)

    tm = 512 if t_s % 512 == 0 else CHUNK
    assert t_p % tm == 0 and t_s % tm == 0 and seq % tm == 0 and tm % CHUNK == 0
    assert seq % CHUNK == 0 and CHUNK % dec == 0 and dec <= CHUNK and tm % dec == 0
    n_p_tiles = t_p // tm
    n_pg = 8 if n_pages % 8 == 0 else 1
    row = lambda a: a.reshape(1, -1)
    bf = lambda a: a.astype(BF16)

    x = jnp.concatenate([x_prompt.reshape(t_p, d), x_sample.reshape(t_s, d)], axis=0)

    pos_p = jnp.arange(seq, dtype=jnp.int32)
    pos_s = past_len + jnp.arange(dec, dtype=jnp.int32)
    cs_tab = jnp.concatenate([_rope_table(pos_p), jnp.tile(_rope_table(pos_s), (tm // dec, 1))], axis=0)
    tiles_per_seq = seq // tm
    cs_map = lambda i: jnp.where(i < n_p_tiles, i % tiles_per_seq, tiles_per_seq)

    wukt3 = bf(jnp.transpose(w_uk, (1, 2, 0)))
    wukt = wukt3.reshape(N_HEADS * D_NOPE, kv_rank)
    w_uv3 = bf(jnp.transpose(w_uv, (1, 0, 2)))

    v_rows = []
    c = kr = cb = krp = rt = rt_past = None
    for layer in range(depth):
        if layer < n_a:
            a = layer
            ws_p, bs_p = _spatial_weights(a_w_s[a], a_b_s[a], min(seq, CHUNK), dg // N_GROUPS)
            ws_s, bs_s = _spatial_weights(a_w_s[a], a_b_s[a], min(dec, CHUNK), dg // N_GROUPS)
            x, v = _gmlp_layer(x, n_p_tiles, tm, row(a_norm_g[a]), bf(a_w_in[a]), row(a_b_in[a]), row(a_sgu_g[a]),
                               bf(jnp.stack([ws_p, ws_s])), jnp.stack([bs_p, bs_s]), bf(a_w_out[a]), row(a_b_out[a]))
            v_rows.append(v.reshape(n_seq, dec, dg))
        else:
            if layer == n_a:
                w_kr2 = bf(jnp.concatenate([w_kr, _swap_halves(w_kr)], axis=1))
                krg2 = row(jnp.concatenate([kr_g, _swap_halves(kr_g)]))
                c, kr, cb, krp, rt = _key_side(x, tm, cs_map, row(kv_norm_g), bf(w_dkv), row(kv_lat_g), w_kr2, krg2,
                                               cs_tab, wukt)
                rt_past = _past_scale(page_table, cache_latent, wukt, n_pg)
                c_new = c[t_p:].reshape(n_seq, dec, kv_rank)
                kr_new = kr[t_p:].reshape(n_seq, dec, D_ROPE)
                rt_new = jnp.transpose(rt[:, t_p:].reshape(N_HEADS, n_seq, dec), (1, 0, 2))
                rt_new = jnp.pad(rt_new, ((0, 0), (0, 0), (0, LANES - dec)))
            b = layer - n_a
            w_uq = b_w_uq[b]
            wn = bf(w_uq[:, :, :D_NOPE].reshape(w_uq.shape[0], N_HEADS * D_NOPE))
            wr = w_uq[:, :, D_NOPE:]
            wr2 = bf(jnp.concatenate([wr, _swap_halves(wr)], axis=2).reshape(w_uq.shape[0], N_HEADS * LANES))
            qrg2 = row(jnp.concatenate([b_qr_g[b], _swap_halves(b_qr_g[b])]))
            qa, qr = _q_side(x, tm, cs_map, row(b_norm_g[b]), bf(b_w_dq[b]), row(b_q_lat_g[b]), wn, wr2,
                             row(b_qn_g[b]), row(kn_g), qrg2, cs_tab, wukt3)
            o_p = _prompt_attention(qa, qr, cb, krp, rt, bsz, seq, QBLOCK, 256 if seq % 256 == 0 else CHUNK, scale)

            def per_seq(a):
                w = a.shape[2]
                return jnp.transpose(a[:, t_p:].reshape(N_HEADS, n_seq, dec, w), (1, 0, 2, 3)).reshape(
                    n_seq, N_HEADS * dec, w)

            o_s = _sample_attention(page_table, per_seq(qa), per_seq(qr), rt_past, c_new, kr_new, rt_new,
                                    cache_latent, cache_krope, n_pg, scale)
            o_s = jnp.transpose(o_s.reshape(n_seq, N_HEADS, dec, kv_rank), (1, 0, 2, 3)).reshape(
                N_HEADS, t_s, kv_rank)
            x = _attn_out(x, o_p, o_s, n_p_tiles, tm, w_uv3, bf(b_w_o[b]))
        i = layer // 2
        if layer % 2 == 0:
            x = _ffn_dense(x, tm, row(f_norm_g[layer]), bf(d_w_gate[i]), bf(d_w_up[i]), bf(d_w_down[i]))
        else:
            x = _moe_layer(x, tm, row(f_norm_g[layer]), bf(m_w_router[i]), bf(m_w_gate[i]), bf(m_w_up[i]),
                           bf(m_w_down[i]), 2)

    return (x[:t_p].reshape(bsz, seq, d), x[t_p:].reshape(n_seq, dec, d),
            c[:t_p].reshape(bsz, seq, kv_rank), kr[:t_p].reshape(bsz, seq, D_ROPE),
            c[t_p:].reshape(n_seq, dec, kv_rank), kr[t_p:].reshape(n_seq, dec, D_ROPE),
            jnp.stack(v_rows))
```

```python
import functools

import jax
import jax.numpy as jnp
from jax import lax
from jax.experimental import pallas as pl
from jax.experimental.pallas import tpu as pltpu

EPS = 1e-6
ROPE_THETA = 10000.0
CHUNK = 128
N_GROUPS = 8
N_HEADS = 8
D_NOPE = 128
D_ROPE = 64
D_V = 128
TOP_K = 2
QBLOCK = 128
LANES = 128
VMEM_LIMIT_BYTES = 60 * 1024 * 1024

BF16 = jnp.bfloat16
F32 = jnp.float32
NT_DIMS = (((1,), (1,)), ((), ()))


def _dot(a, b):
    return jnp.dot(a, b, preferred_element_type=F32)


def _dot_nt(a, b):
    return lax.dot_general(a, b, NT_DIMS, preferred_element_type=F32)


def _rms(x, g):
    ms = jnp.mean(x * x, axis=-1, keepdims=True)
    return x * lax.rsqrt(ms + EPS) * g


def _rope_pairs(a, g2, cs):
    ms = jnp.sum(a * a, axis=-1, keepdims=True) * (1.0 / LANES)
    t = a * lax.rsqrt(ms + EPS) * g2 * cs
    return t + pltpu.roll(t, D_ROPE, 1)


def _resident(shape):
    nd = len(shape)
    return pl.BlockSpec(shape, lambda *_: (0,) * nd, pipeline_mode=pl.Buffered(1))


def _params(sem):
    return pltpu.CompilerParams(dimension_semantics=sem, vmem_limit_bytes=VMEM_LIMIT_BYTES)


def _gmlp_kernel(n_p_tiles, x_ref, ng_ref, win_ref, bin_ref, sg_ref, ws_ref, bs_ref, wout_ref, bout_ref,
                 xo_ref, v_ref, zu_ref, vb_ref, gated_ref):
    i = pl.program_id(0)
    x = x_ref[...]
    h = _rms(x, ng_ref[...]).astype(BF16)
    dg = sg_ref.shape[1]
    zu_ref[...] = jax.nn.gelu(_dot(h, win_ref[:, :dg]) + bin_ref[:, :dg])
    v = _rms(jax.nn.gelu(_dot(h, win_ref[:, dg:]) + bin_ref[:, dg:]), sg_ref[...])
    vb_ref[...] = v.astype(BF16)

    @pl.when(i >= n_p_tiles)
    def _():
        v_ref[...] = v

    gw = dg // N_GROUPS
    for c in range(x.shape[0] // CHUNK):
        rows = slice(c * CHUNK, (c + 1) * CHUNK)
        for g in range(N_GROUPS):
            cols = slice(g * gw, (g + 1) * gw)
            mix = _dot(ws_ref[0, g], vb_ref[rows, cols]) + bs_ref[0, :, cols]
            gated_ref[rows, cols] = (zu_ref[rows, cols] * mix).astype(BF16)
    xo_ref[...] = x + _dot(gated_ref[...], wout_ref[...]) + bout_ref[...]


def _gmlp_layer(x, n_p_tiles, tm, ng, w_in, b_in, sg, ws_eff, bs_eff, w_out, b_out):
    t, d = x.shape
    dg = sg.shape[1]
    n_tiles = t // tm
    t_s = t - n_p_tiles * tm
    sel = lambda i: jnp.where(i >= n_p_tiles, 1, 0)
    return pl.pallas_call(
        functools.partial(_gmlp_kernel, n_p_tiles),
        grid=(n_tiles,),
        in_specs=[
            pl.BlockSpec((tm, d), lambda i: (i, 0)),
            _resident(ng.shape), _resident(w_in.shape), _resident(b_in.shape), _resident(sg.shape),
            pl.BlockSpec((1,) + ws_eff.shape[1:], lambda i: (sel(i), 0, 0, 0)),
            pl.BlockSpec((1,) + bs_eff.shape[1:], lambda i: (sel(i), 0, 0)),
            _resident(w_out.shape), _resident(b_out.shape),
        ],
        out_specs=[
            pl.BlockSpec((tm, d), lambda i: (i, 0)),
            pl.BlockSpec((tm, dg), lambda i: (jnp.maximum(i - n_p_tiles, 0), 0)),
        ],
        out_shape=[jax.ShapeDtypeStruct((t, d), F32), jax.ShapeDtypeStruct((t_s, dg), F32)],
        scratch_shapes=[pltpu.VMEM((tm, dg), F32), pltpu.VMEM((tm, dg), BF16), pltpu.VMEM((tm, dg), BF16)],
        compiler_params=_params(("arbitrary",)),
        name="gmlp_layer",
    )(x, ng, w_in, b_in, sg, ws_eff, bs_eff, w_out, b_out)


def _ffn_kernel(x_ref, ng_ref, wg_ref, wu_ref, wd_ref, xo_ref):
    x = x_ref[...]
    h = _rms(x, ng_ref[...]).astype(BF16)
    t = (jax.nn.silu(_dot(h, wg_ref[...])) * _dot(h, wu_ref[...])).astype(BF16)
    xo_ref[...] = x + _dot(t, wd_ref[...])


def _ffn_dense(x, tm, ng, wg, wu, wd):
    t, d = x.shape
    return pl.pallas_call(
        _ffn_kernel,
        grid=(t // tm,),
        in_specs=[pl.BlockSpec((tm, d), lambda i: (i, 0)),
                  _resident(ng.shape), _resident(wg.shape), _resident(wu.shape), _resident(wd.shape)],
        out_specs=pl.BlockSpec((tm, d), lambda i: (i, 0)),
        out_shape=jax.ShapeDtypeStruct((t, d), F32),
        compiler_params=_params(("arbitrary",)),
        name="ffn_dense",
    )(x, ng, wg, wu, wd)


MOE_ROW_TILE = 512
DISPATCH_TOKENS = 1024
COMBINE_TOKENS = 256


def _router_kernel(x_ref, ng_ref, wrt_ref, tri_ref, idx_ref, gate_ref, rank_ref, cnt_ref, run_ref):
    @pl.when(pl.program_id(0) == 0)
    def _():
        run_ref[...] = jnp.zeros_like(run_ref)

    hb = _rms(x_ref[...], ng_ref[...]).astype(BF16)
    logits = _dot_nt(wrt_ref[...], hb)
    ne = logits.shape[0]
    ex = jnp.exp(logits - jnp.max(logits, axis=0, keepdims=True))
    probs = ex / jnp.sum(ex, axis=0, keepdims=True)
    row = lax.broadcasted_iota(jnp.int32, probs.shape, 0)
    m1 = jnp.max(probs, axis=0, keepdims=True)
    i1 = jnp.min(jnp.where(probs == m1, row, ne), axis=0, keepdims=True)
    rest = jnp.where(row == i1, -1.0, probs)
    m2 = jnp.max(rest, axis=0, keepdims=True)
    i2 = jnp.min(jnp.where(rest == m2, row, ne), axis=0, keepdims=True)
    den = m1 + m2
    oh1 = row == i1
    oh2 = row == i2
    oh = jnp.where(oh1 | oh2, 1.0, 0.0)
    rank = _dot(oh.astype(BF16), tri_ref[...]) + run_ref[:, :1]
    idx_ref[...] = jnp.concatenate([i1, i2], axis=0)
    gate_ref[...] = jnp.concatenate([m1 / den, m2 / den], axis=0)
    rank_ref[...] = jnp.concatenate([jnp.sum(jnp.where(oh1, rank, 0.0), axis=0, keepdims=True),
                                     jnp.sum(jnp.where(oh2, rank, 0.0), axis=0, keepdims=True)],
                                    axis=0).astype(jnp.int32)
    run = run_ref[...] + jnp.sum(oh, axis=1, keepdims=True)
    run_ref[...] = run
    cnt_ref[...] = run.astype(jnp.int32)


def _router(x, tm, ng, wrt):
    t, d = x.shape
    ne = wrt.shape[0]
    tri = jnp.triu(jnp.ones((tm, tm), BF16), k=1)
    tok = lambda i: (0, i)
    return pl.pallas_call(
        _router_kernel,
        grid=(t // tm,),
        in_specs=[pl.BlockSpec((tm, d), lambda i: (i, 0)), _resident(ng.shape), _resident(wrt.shape),
                  _resident(tri.shape)],
        out_specs=[pl.BlockSpec((TOP_K, tm), tok), pl.BlockSpec((TOP_K, tm), tok), pl.BlockSpec((TOP_K, tm), tok),
                   pl.BlockSpec((ne, LANES), lambda i: (0, 0))],
        out_shape=[jax.ShapeDtypeStruct((TOP_K, t), jnp.int32), jax.ShapeDtypeStruct((TOP_K, t), F32),
                   jax.ShapeDtypeStruct((TOP_K, t), jnp.int32), jax.ShapeDtypeStruct((ne, LANES), jnp.int32)],
        scratch_shapes=[pltpu.VMEM((ne, LANES), F32)],
        compiler_params=_params(("arbitrary",)),
        name="moe_router",
    )(x, ng, wrt, tri)


def _row_copy(src_ref, src_row, dst_ref, dst_row, sem):
    return pltpu.make_async_copy(src_ref.at[pl.ds(src_row, 1)], dst_ref.at[pl.ds(dst_row, 1)], sem)


def _dispatch_kernel(pos_ref, x_hbm, xs_in_hbm, xs_hbm, sem):
    del xs_in_hbm
    n = pos_ref.shape[1]
    base = pl.program_id(0) * n

    def issue(r, carry):
        for k in range(TOP_K):
            _row_copy(x_hbm, base + r, xs_hbm, pos_ref[k, r], sem).start()
        return carry

    lax.fori_loop(0, n, issue, 0, unroll=8)
    for k in range(TOP_K):
        pltpu.make_async_copy(x_hbm.at[pl.ds(0, n)], xs_hbm.at[pl.ds(0, n)], sem).wait()


def _dispatch(x, pos, n_rows):
    t, d = x.shape
    n = DISPATCH_TOKENS if t % DISPATCH_TOKENS == 0 else LANES
    return pl.pallas_call(
        _dispatch_kernel,
        grid=(t // n,),
        in_specs=[pl.BlockSpec((TOP_K, n), lambda i: (0, i), memory_space=pltpu.SMEM),
                  pl.BlockSpec(memory_space=pl.ANY), pl.BlockSpec(memory_space=pl.ANY)],
        out_specs=pl.BlockSpec(memory_space=pl.ANY),
        out_shape=jax.ShapeDtypeStruct((n_rows, d), F32),
        scratch_shapes=[pltpu.SemaphoreType.DMA(())],
        input_output_aliases={2: 0},
        compiler_params=_params(("arbitrary",)),
        name="moe_dispatch",
    )(pos, x, jnp.zeros((n_rows, d), F32))


def _grouped_ffn_kernel(layer, te_ref, nu_ref, xs_ref, ng_ref, wg_ref, wu_ref, wd_ref, ys_ref):
    del layer, te_ref
    i = pl.program_id(0)

    @pl.when(i < nu_ref[0])
    def _():
        h = _rms(xs_ref[...], ng_ref[...]).astype(BF16)
        t = (jax.nn.silu(_dot(h, wg_ref[0, 0])) * _dot(h, wu_ref[0, 0])).astype(BF16)
        ys_ref[...] = _dot(t, wd_ref[0, 0])

    @pl.when(i >= nu_ref[0])
    def _():
        ys_ref[...] = jnp.zeros_like(ys_ref)


def _grouped_ffn(xs, tile_expert, n_used, layer, ng, wg, wu, wd):
    n_rows, d = xs.shape
    dff = wg.shape[3]
    tmg = MOE_ROW_TILE
    wspec = lambda shape: pl.BlockSpec((1, 1) + shape, lambda i, te, nu: (layer, te[i], 0, 0),
                                       pipeline_mode=pl.Buffered(1))
    return pl.pallas_call(
        functools.partial(_grouped_ffn_kernel, layer),
        grid_spec=pltpu.PrefetchScalarGridSpec(
            num_scalar_prefetch=2,
            grid=(n_rows // tmg,),
            in_specs=[pl.BlockSpec((tmg, d), lambda i, te, nu: (i, 0)),
                      pl.BlockSpec(ng.shape, lambda i, te, nu: (0, 0), pipeline_mode=pl.Buffered(1)),
                      wspec((d, dff)), wspec((d, dff)), wspec((dff, d))],
            out_specs=pl.BlockSpec((tmg, d), lambda i, te, nu: (i, 0))),
        out_shape=jax.ShapeDtypeStruct((n_rows, d), F32),
        compiler_params=_params(("arbitrary",)),
        name="moe_grouped_ffn",
    )(tile_expert, n_used, xs, ng, wg, wu, wd)


def _combine_kernel(pos_ref, x_ref, gate_ref, ys_hbm, xo_ref, buf_ref, sem):
    n = x_ref.shape[0]

    def issue(r, carry):
        for k in range(TOP_K):
            _row_copy(ys_hbm, pos_ref[k, r], buf_ref.at[k], r, sem).start()
        return carry

    lax.fori_loop(0, n, issue, 0, unroll=8)
    for k in range(TOP_K):
        pltpu.make_async_copy(ys_hbm.at[pl.ds(0, n)], buf_ref.at[k], sem).wait()
    g = gate_ref[...]
    xo_ref[...] = x_ref[...] + (g[:, 0:1] * buf_ref[0] + g[:, 1:2] * buf_ref[1])


def _combine(x, pos, gates, ys):
    t, d = x.shape
    n = COMBINE_TOKENS if t % COMBINE_TOKENS == 0 else LANES
    return pl.pallas_call(
        _combine_kernel,
        grid=(t // n,),
        in_specs=[pl.BlockSpec((TOP_K, n), lambda i: (0, i), memory_space=pltpu.SMEM),
                  pl.BlockSpec((n, d), lambda i: (i, 0)), pl.BlockSpec((n, TOP_K), lambda i: (i, 0)),
                  pl.BlockSpec(memory_space=pl.ANY)],
        out_specs=pl.BlockSpec((n, d), lambda i: (i, 0)),
        out_shape=jax.ShapeDtypeStruct((t, d), F32),
        scratch_shapes=[pltpu.VMEM((TOP_K, n, d), F32), pltpu.SemaphoreType.DMA(())],
        compiler_params=_params(("arbitrary",)),
        name="moe_combine",
    )(pos, x, gates, ys)


def _moe_layer(x, tm, layer, ng, wrt, wg, wu, wd):
    t, d = x.shape
    ne = wrt.shape[0]
    tmg = MOE_ROW_TILE
    n_rows = -(-(t * TOP_K) // tmg) * tmg + ne * tmg
    idx, gate, rank, cnt = _router(x, tm, ng, wrt)
    padded = (cnt[:, 0] + tmg - 1) // tmg * tmg
    ends = jnp.cumsum(padded)
    starts = ends - padded
    pos = jnp.take(starts, idx) + rank
    tile_start = jnp.arange(n_rows // tmg, dtype=jnp.int32) * tmg
    tile_expert = jnp.minimum(jnp.sum(tile_start[:, None] >= ends[None, :], axis=1), ne - 1).astype(jnp.int32)
    n_used = (ends[-1:] // tmg).astype(jnp.int32)
    xs = _dispatch(x, pos, n_rows)
    ys = _grouped_ffn(xs, tile_expert, n_used, layer, ng, wg, wu, wd)
    return _combine(x, pos, gate.T, ys)


def _keyside_kernel(x_ref, ng_ref, wdkv_ref, lg_ref, wkr_ref, krg_ref, cs_ref, wukt_ref,
                    c_ref, kr_ref, cb_ref, krp_ref, rt_ref):
    h = _rms(x_ref[...], ng_ref[...]).astype(BF16)
    c = _rms(_dot(h, wdkv_ref[...]), lg_ref[...])
    cb = c.astype(BF16)
    c_ref[...] = c
    cb_ref[...] = cb
    full = _rope_pairs(_dot(h, wkr_ref[...]), krg_ref[...], cs_ref[...])
    kr_ref[...] = full[:, :D_ROPE]
    lane = lax.broadcasted_iota(jnp.int32, full.shape, 1)
    krp_ref[...] = jnp.where(lane < D_ROPE, full, 0.0).astype(BF16)
    rt_ref[...] = _key_rms_scale_t(wukt_ref[...], cb)


def _key_rms_scale_t(wukt, cb):
    kt = _dot_nt(wukt, cb)
    ssq = jnp.sum((kt * kt).reshape(N_HEADS, D_NOPE, kt.shape[1]), axis=1)
    return lax.rsqrt(ssq * (1.0 / D_NOPE) + EPS)


def _key_side(x, tm, cs_map, ng, w_dkv, lg, w_kr2, krg2, cs_tab, wukt):
    t, d = x.shape
    r = w_dkv.shape[1]
    row = lambda i: (i, 0)
    return pl.pallas_call(
        _keyside_kernel,
        grid=(t // tm,),
        in_specs=[pl.BlockSpec((tm, d), row),
                  _resident(ng.shape), _resident(w_dkv.shape), _resident(lg.shape),
                  _resident(w_kr2.shape), _resident(krg2.shape),
                  pl.BlockSpec((tm, LANES), lambda i: (cs_map(i), 0)),
                  _resident(wukt.shape)],
        out_specs=[pl.BlockSpec((tm, r), row), pl.BlockSpec((tm, D_ROPE), row), pl.BlockSpec((tm, r), row),
                   pl.BlockSpec((tm, LANES), row), pl.BlockSpec((N_HEADS, tm), lambda i: (0, i))],
        out_shape=[jax.ShapeDtypeStruct((t, r), F32), jax.ShapeDtypeStruct((t, D_ROPE), F32),
                   jax.ShapeDtypeStruct((t, r), BF16), jax.ShapeDtypeStruct((t, LANES), BF16),
                   jax.ShapeDtypeStruct((N_HEADS, t), F32)],
        compiler_params=_params(("arbitrary",)),
        name="key_side",
    )(x, ng, w_dkv, lg, w_kr2, krg2, cs_tab, wukt)


def _past_scale_kernel(n_pg, pt_ref, *refs):
    wukt_ref = refs[0]
    pages = refs[1:1 + n_pg]
    rt_ref = refs[1 + n_pg]
    cblk_ref = refs[2 + n_pg]
    for g in range(n_pg):
        cblk_ref[g * CHUNK:(g + 1) * CHUNK, :] = pages[g][0].astype(BF16)
    rt_ref[0] = _key_rms_scale_t(wukt_ref[...], cblk_ref[...])


def _page_specs(cache, n_pages, n_pg):
    page, width = cache.shape[1:]
    return [pl.BlockSpec((1, page, width),
                         functools.partial(lambda g, b, j, pt: (pt[b * n_pages + j * n_pg + g], 0, 0), g))
            for g in range(n_pg)]


def _past_scale(page_table, cache_latent, wukt, n_pg):
    n_seq, n_pages = page_table.shape
    page, r = cache_latent.shape[1:]
    gk = n_pg * page
    return pl.pallas_call(
        functools.partial(_past_scale_kernel, n_pg),
        grid_spec=pltpu.PrefetchScalarGridSpec(
            num_scalar_prefetch=1,
            grid=(n_seq, n_pages // n_pg),
            in_specs=[pl.BlockSpec(wukt.shape, lambda b, j, pt: (0, 0), pipeline_mode=pl.Buffered(1))]
            + _page_specs(cache_latent, n_pages, n_pg),
            out_specs=pl.BlockSpec((1, N_HEADS, gk), lambda b, j, pt: (b, 0, j)),
            scratch_shapes=[pltpu.VMEM((gk, r), BF16)]),
        out_shape=jax.ShapeDtypeStruct((n_seq, N_HEADS, n_pages * page), F32),
        compiler_params=_params(("arbitrary", "arbitrary")),
        name="past_key_scale",
    )(page_table.reshape(-1), wukt, *([cache_latent] * n_pg))


def _qside_kernel(x_ref, ng_ref, wdq_ref, qlg_ref, wn_ref, wr_ref, qng_ref, kng_ref, qrg_ref, cs_ref, wukt_ref,
                  qa_ref, qr_ref):
    h = _rms(x_ref[...], ng_ref[...]).astype(BF16)
    cq = _rms(_dot(h, wdq_ref[...]), qlg_ref[...]).astype(BF16)
    qn_all = _dot(cq, wn_ref[...])
    qr_all = _dot(cq, wr_ref[...])
    cs = cs_ref[...]
    for hh in range(N_HEADS):
        cols = slice(hh * LANES, (hh + 1) * LANES)
        qn = (_rms(qn_all[:, cols], qng_ref[...]) * kng_ref[...]).astype(BF16)
        qa_ref[hh] = _dot(qn, wukt_ref[hh]).astype(BF16)
        qr_ref[hh] = _rope_pairs(qr_all[:, cols], qrg_ref[...], cs).astype(BF16)


def _q_side(x, tm, cs_map, ng, w_dq, qlg, wn, wr2, qng, kng, qrg2, cs_tab, wukt3):
    t, d = x.shape
    r = wukt3.shape[2]
    return pl.pallas_call(
        _qside_kernel,
        grid=(t // tm,),
        in_specs=[pl.BlockSpec((tm, d), lambda i: (i, 0)),
                  _resident(ng.shape), _resident(w_dq.shape), _resident(qlg.shape), _resident(wn.shape),
                  _resident(wr2.shape), _resident(qng.shape), _resident(kng.shape), _resident(qrg2.shape),
                  pl.BlockSpec((tm, LANES), lambda i: (cs_map(i), 0)),
                  _resident(wukt3.shape)],
        out_specs=[pl.BlockSpec((N_HEADS, tm, r), lambda i: (0, i, 0)),
                   pl.BlockSpec((N_HEADS, tm, LANES), lambda i: (0, i, 0))],
        out_shape=[jax.ShapeDtypeStruct((N_HEADS, t, r), BF16), jax.ShapeDtypeStruct((N_HEADS, t, LANES), BF16)],
        compiler_params=_params(("arbitrary",)),
        name="q_side",
    )(x, ng, w_dq, qlg, wn, wr2, qng, kng, qrg2, cs_tab, wukt3)


def _softmax_step(s, cb, m, l, acc):
    m_new = jnp.maximum(m, jnp.max(s, axis=1, keepdims=True))
    alpha = jnp.exp(m - m_new)
    p = jnp.exp(s - m_new)
    l_new = alpha * l + jnp.sum(p, axis=1, keepdims=True)
    acc_new = alpha * acc + _dot(p.astype(BF16), cb)
    return m_new, l_new, acc_new


def _scores(q, qr, cb, krb, rt, scale):
    nq = q.shape[0] // N_HEADS
    nk = cb.shape[0]
    s1 = _dot_nt(q, cb).reshape(N_HEADS, nq, nk) * rt[:, None, :]
    return (s1.reshape(N_HEADS * nq, nk) + _dot_nt(qr, krb)) * scale


def _prompt_attn_kernel(bq, bk, scale, qa_ref, qr_ref, cb_ref, krp_ref, rt_ref, o_ref):
    i = pl.program_id(1)
    r = qa_ref.shape[2]
    q = qa_ref[...].reshape(N_HEADS * bq, r)
    qr = qr_ref[...].reshape(N_HEADS * bq, LANES)
    qpos = i * bq + lax.broadcasted_iota(jnp.int32, (N_HEADS, bq, bk), 1).reshape(N_HEADS * bq, bk)
    kidx = lax.broadcasted_iota(jnp.int32, (N_HEADS * bq, bk), 1)

    def body(j, carry):
        k0 = pl.multiple_of(j * bk, bk)
        cb = cb_ref[pl.ds(k0, bk), :]
        s = _scores(q, qr, cb, krp_ref[pl.ds(k0, bk), :], rt_ref[:, pl.ds(k0, bk)], scale)
        s = jnp.where(kidx + k0 <= qpos, s, -jnp.inf)
        return _softmax_step(s, cb, *carry)

    init = (jnp.full((N_HEADS * bq, 1), -jnp.inf, F32), jnp.zeros((N_HEADS * bq, 1), F32),
            jnp.zeros((N_HEADS * bq, r), F32))
    _, l, acc = lax.fori_loop(0, (i * bq + bq + bk - 1) // bk, body, init)
    o_ref[...] = (acc / l).reshape(N_HEADS, bq, r).astype(BF16)


def _prompt_attention(qa, qr, cb, krp, rt, bsz, seq, bq, bk, scale):
    r = cb.shape[1]
    nq = seq // bq
    qmap = lambda b, i: (0, b * nq + i, 0)
    return pl.pallas_call(
        functools.partial(_prompt_attn_kernel, bq, bk, scale),
        grid=(bsz, nq),
        in_specs=[pl.BlockSpec((N_HEADS, bq, r), qmap), pl.BlockSpec((N_HEADS, bq, LANES), qmap),
                  pl.BlockSpec((seq, r), lambda b, i: (b, 0)), pl.BlockSpec((seq, LANES), lambda b, i: (b, 0)),
                  pl.BlockSpec((N_HEADS, seq), lambda b, i: (0, b))],
        out_specs=pl.BlockSpec((N_HEADS, bq, r), qmap),
        out_shape=jax.ShapeDtypeStruct((N_HEADS, bsz * seq, r), BF16),
        compiler_params=_params(("arbitrary", "arbitrary")),
        name="prompt_attention",
    )(qa, qr, cb, krp, rt)


def _sample_attn_kernel(n_pg, n_new, scale, pt_ref, *refs):
    qa_ref, qr_ref, rtp_ref, cn_ref, krn_ref, rtn_ref = refs[:6]
    cpages = refs[6:6 + n_pg]
    kpages = refs[6 + n_pg:6 + 2 * n_pg]
    o_ref, m_ref, l_ref, acc_ref, cblk_ref, kblk_ref = refs[6 + 2 * n_pg:]
    j = pl.program_id(1)

    @pl.when(j == 0)
    def _():
        m_ref[...] = jnp.full_like(m_ref, -jnp.inf)
        l_ref[...] = jnp.zeros_like(l_ref)
        acc_ref[...] = jnp.zeros_like(acc_ref)

    for g in range(n_pg):
        cblk_ref[g * CHUNK:(g + 1) * CHUNK, :] = cpages[g][0].astype(BF16)
        kblk_ref[g * CHUNK:(g + 1) * CHUNK, :] = kpages[g][0].astype(BF16)
    q = qa_ref[0]
    qr = qr_ref[0][:, :D_ROPE]
    cb = cblk_ref[...]
    s = _scores(q, qr, cb, kblk_ref[...], rtp_ref[0], scale)
    m, l, acc = _softmax_step(s, cb, m_ref[...], l_ref[...], acc_ref[...])
    m_ref[...] = m
    l_ref[...] = l
    acc_ref[...] = acc

    @pl.when(j == pl.num_programs(1) - 1)
    def _():
        pad = CHUNK - n_new
        cn = jnp.concatenate([cn_ref[0], jnp.zeros((pad, cn_ref.shape[2]), F32)], axis=0).astype(BF16)
        krn = jnp.concatenate([krn_ref[0], jnp.zeros((pad, D_ROPE), F32)], axis=0).astype(BF16)
        sn = _scores(q, qr, cn, krn, rtn_ref[0], scale)
        nrow = q.shape[0]
        qi = lax.broadcasted_iota(jnp.int32, (N_HEADS, nrow // N_HEADS, CHUNK), 1).reshape(nrow, CHUNK)
        ki = lax.broadcasted_iota(jnp.int32, (nrow, CHUNK), 1)
        sn = jnp.where(ki <= qi, sn, -jnp.inf)
        _, l2, acc2 = _softmax_step(sn, cn, m, l, acc)
        o_ref[0] = (acc2 / l2).astype(BF16)


def _sample_attention(page_table, qa_s, qr_s, rt_past, c_new, kr_new, rt_new, cache_latent, cache_krope,
                      n_pg, scale):
    n_seq, n_pages = page_table.shape
    page, r = cache_latent.shape[1:]
    nrow = qa_s.shape[1]
    n_new = c_new.shape[1]
    gk = n_pg * page
    seq_blk = lambda shape: pl.BlockSpec((1,) + shape, lambda b, j, pt: (b, 0, 0))
    return pl.pallas_call(
        functools.partial(_sample_attn_kernel, n_pg, n_new, scale),
        grid_spec=pltpu.PrefetchScalarGridSpec(
            num_scalar_prefetch=1,
            grid=(n_seq, n_pages // n_pg),
            in_specs=[seq_blk((nrow, r)), seq_blk((nrow, LANES)),
                      pl.BlockSpec((1, N_HEADS, gk), lambda b, j, pt: (b, 0, j)),
                      seq_blk((n_new, r)), seq_blk((n_new, D_ROPE)), seq_blk((N_HEADS, LANES))]
            + _page_specs(cache_latent, n_pages, n_pg) + _page_specs(cache_krope, n_pages, n_pg),
            out_specs=seq_blk((nrow, r)),
            scratch_shapes=[pltpu.VMEM((nrow, 1), F32), pltpu.VMEM((nrow, 1), F32), pltpu.VMEM((nrow, r), F32),
                            pltpu.VMEM((gk, r), BF16), pltpu.VMEM((gk, D_ROPE), BF16)]),
        out_shape=jax.ShapeDtypeStruct((n_seq, nrow, r), BF16),
        compiler_params=_params(("arbitrary", "arbitrary")),
        name="sample_attention",
    )(page_table.reshape(-1), qa_s, qr_s, rt_past, c_new, kr_new, rt_new,
      *([cache_latent] * n_pg), *([cache_krope] * n_pg))


def _attn_out_kernel(n_p_tiles, x_ref, op_ref, os_ref, wuv_ref, wo_ref, xo_ref):
    i = pl.program_id(0)

    def project(o_ref):
        o = jnp.concatenate([_dot(o_ref[hh], wuv_ref[hh]) for hh in range(N_HEADS)], axis=1).astype(BF16)
        xo_ref[...] = x_ref[...] + _dot(o, wo_ref[...])

    @pl.when(i < n_p_tiles)
    def _():
        project(op_ref)

    @pl.when(i >= n_p_tiles)
    def _():
        project(os_ref)


def _attn_out(x, o_p, o_s, n_p_tiles, tm, w_uv3, w_o):
    t, d = x.shape
    r = o_p.shape[2]
    return pl.pallas_call(
        functools.partial(_attn_out_kernel, n_p_tiles),
        grid=(t // tm,),
        in_specs=[pl.BlockSpec((tm, d), lambda i: (i, 0)),
                  pl.BlockSpec((N_HEADS, tm, r), lambda i: (0, jnp.minimum(i, n_p_tiles - 1), 0)),
                  pl.BlockSpec((N_HEADS, tm, r), lambda i: (0, jnp.maximum(i - n_p_tiles, 0), 0)),
                  _resident(w_uv3.shape), _resident(w_o.shape)],
        out_specs=pl.BlockSpec((tm, d), lambda i: (i, 0)),
        out_shape=jax.ShapeDtypeStruct((t, d), F32),
        compiler_params=_params(("arbitrary",)),
        name="attn_out",
    )(x, o_p, o_s, w_uv3, w_o)


def _swap_halves(a):
    half = a.shape[-1] // 2
    return jnp.concatenate([a[..., half:], a[..., :half]], axis=-1)


def _rope_table(pos):
    half = D_ROPE // 2
    inv = ROPE_THETA ** (-jnp.arange(half, dtype=F32) / half)
    ang = pos.astype(F32)[:, None] * inv[None, :]
    cos, sin = jnp.cos(ang), jnp.sin(ang)
    return jnp.concatenate([cos, cos, -sin, sin], axis=1)


def _spatial_weights(w_s, b_s, length, gw):
    reps = CHUNK // length
    w = w_s[:, :length, :length] * jnp.tril(jnp.ones((length, length), w_s.dtype))
    w_eff = jnp.einsum("st,gpq->gsptq", jnp.eye(reps, dtype=w.dtype), w).reshape(N_GROUPS, CHUNK, CHUNK)
    b_eff = jnp.repeat(jnp.tile(b_s[:, :length].T, (reps, 1)), gw, axis=1)
    return w_eff, b_eff


def kernel(x_prompt, x_sample, cache_latent, cache_krope, page_table, a_norm_g, a_w_in, a_b_in, a_sgu_g, a_w_s, a_b_s, a_w_out, a_b_out, kv_norm_g, w_dkv, kv_lat_g, w_kr, kr_g, kn_g, w_uk, w_uv, b_norm_g, b_w_dq, b_q_lat_g, b_w_uq, b_qn_g, b_qr_g, b_w_o, f_norm_g, d_w_gate, d_w_up, d_w_down, m_w_router, m_w_gate, m_w_up, m_w_down):
    bsz, seq, d = x_prompt.shape
    n_seq, dec = x_sample.shape[:2]
    t_p, t_s = bsz * seq, n_seq * dec
    n_pages = page_table.shape[1]
    past_len = n_pages * cache_latent.shape[1]
    depth = f_norm_g.shape[0]
    n_a = a_w_in.shape[0]
    dg = a_sgu_g.shape[1]
    kv_rank = w_dkv.shape[1]
    scale = float((D_NOPE + D_ROPE) ** -0.5)

    tm = 512 if t_s % 512 == 0 else CHUNK
    assert t_p % tm == 0 and t_s % tm == 0 and seq % tm == 0 and tm % CHUNK == 0
    assert seq % CHUNK == 0 and CHUNK % dec == 0 and dec <= CHUNK and tm % dec == 0
    n_p_tiles = t_p // tm
    n_pg = 8 if n_pages % 8 == 0 else 1
    row = lambda a: a.reshape(1, -1)
    bf = lambda a: a.astype(BF16)

    x = jnp.concatenate([x_prompt.reshape(t_p, d), x_sample.reshape(t_s, d)], axis=0)

    pos_p = jnp.arange(seq, dtype=jnp.int32)
    pos_s = past_len + jnp.arange(dec, dtype=jnp.int32)
    cs_tab = jnp.concatenate([_rope_table(pos_p), jnp.tile(_rope_table(pos_s), (tm // dec, 1))], axis=0)
    tiles_per_seq = seq // tm
    cs_map = lambda i: jnp.where(i < n_p_tiles, i % tiles_per_seq, tiles_per_seq)

    wukt3 = bf(jnp.transpose(w_uk, (1, 2, 0)))
    wukt = wukt3.reshape(N_HEADS * D_NOPE, kv_rank)
    w_uv3 = bf(jnp.transpose(w_uv, (1, 0, 2)))
    m_wg, m_wu, m_wd = bf(m_w_gate), bf(m_w_up), bf(m_w_down)

    v_rows = []
    c = kr = cb = krp = rt = rt_past = None
    for layer in range(depth):
        if layer < n_a:
            a = layer
            ws_p, bs_p = _spatial_weights(a_w_s[a], a_b_s[a], min(seq, CHUNK), dg // N_GROUPS)
            ws_s, bs_s = _spatial_weights(a_w_s[a], a_b_s[a], min(dec, CHUNK), dg // N_GROUPS)
            x, v = _gmlp_layer(x, n_p_tiles, tm, row(a_norm_g[a]), bf(a_w_in[a]), row(a_b_in[a]), row(a_sgu_g[a]),
                               bf(jnp.stack([ws_p, ws_s])), jnp.stack([bs_p, bs_s]), bf(a_w_out[a]), row(a_b_out[a]))
            v_rows.append(v.reshape(n_seq, dec, dg))
        else:
            if layer == n_a:
                w_kr2 = bf(jnp.concatenate([w_kr, _swap_halves(w_kr)], axis=1))
                krg2 = row(jnp.concatenate([kr_g, _swap_halves(kr_g)]))
                c, kr, cb, krp, rt = _key_side(x, tm, cs_map, row(kv_norm_g), bf(w_dkv), row(kv_lat_g), w_kr2, krg2,
                                               cs_tab, wukt)
                rt_past = _past_scale(page_table, cache_latent, wukt, n_pg)
                c_new = c[t_p:].reshape(n_seq, dec, kv_rank)
                kr_new = kr[t_p:].reshape(n_seq, dec, D_ROPE)
                rt_new = jnp.transpose(rt[:, t_p:].reshape(N_HEADS, n_seq, dec), (1, 0, 2))
                rt_new = jnp.pad(rt_new, ((0, 0), (0, 0), (0, LANES - dec)))
            b = layer - n_a
            w_uq = b_w_uq[b]
            wn = bf(w_uq[:, :, :D_NOPE].reshape(w_uq.shape[0], N_HEADS * D_NOPE))
            wr = w_uq[:, :, D_NOPE:]
            wr2 = bf(jnp.concatenate([wr, _swap_halves(wr)], axis=2).reshape(w_uq.shape[0], N_HEADS * LANES))
            qrg2 = row(jnp.concatenate([b_qr_g[b], _swap_halves(b_qr_g[b])]))
            qa, qr = _q_side(x, tm, cs_map, row(b_norm_g[b]), bf(b_w_dq[b]), row(b_q_lat_g[b]), wn, wr2,
                             row(b_qn_g[b]), row(kn_g), qrg2, cs_tab, wukt3)
            o_p = _prompt_attention(qa, qr, cb, krp, rt, bsz, seq, QBLOCK, 256 if seq % 256 == 0 else CHUNK, scale)

            def per_seq(a):
                w = a.shape[2]
                return jnp.transpose(a[:, t_p:].reshape(N_HEADS, n_seq, dec, w), (1, 0, 2, 3)).reshape(
                    n_seq, N_HEADS * dec, w)

            o_s = _sample_attention(page_table, per_seq(qa), per_seq(qr), rt_past, c_new, kr_new, rt_new,
                                    cache_latent, cache_krope, n_pg, scale)
            o_s = jnp.transpose(o_s.reshape(n_seq, N_HEADS, dec, kv_rank), (1, 0, 2, 3)).reshape(
                N_HEADS, t_s, kv_rank)
            x = _attn_out(x, o_p, o_s, n_p_tiles, tm, w_uv3, bf(b_w_o[b]))
        i = layer // 2
        if layer % 2 == 0:
            x = _ffn_dense(x, tm, row(f_norm_g[layer]), bf(d_w_gate[i]), bf(d_w_up[i]), bf(d_w_down[i]))
        else:
            x = _moe_layer(x, tm, i, row(f_norm_g[layer]), bf(m_w_router[i].T), m_wg, m_wu, m_wd)

    return (x[:t_p].reshape(bsz, seq, d), x[t_p:].reshape(n_seq, dec, d),
            c[:t_p].reshape(bsz, seq, kv_rank), kr[:t_p].reshape(bsz, seq, D_ROPE),
            c[t_p:].reshape(n_seq, dec, kv_rank), kr[t_p:].reshape(n_seq, dec, D_ROPE),
            jnp.stack(v_rows))
```

```python
import functools

import jax
import jax.numpy as jnp
from jax import lax
from jax.experimental import pallas as pl
from jax.experimental.pallas import tpu as pltpu

EPS = 1e-6
ROPE_THETA = 10000.0
CHUNK = 128
N_GROUPS = 8
N_HEADS = 8
D_NOPE = 128
D_ROPE = 64
D_V = 128
TOP_K = 2
QBLOCK = 128
LANES = 128
VMEM_LIMIT_BYTES = 60 * 1024 * 1024

BF16 = jnp.bfloat16
F32 = jnp.float32
NT_DIMS = (((1,), (1,)), ((), ()))


def _dot(a, b):
    return jnp.dot(a, b, preferred_element_type=F32)


def _dot_nt(a, b):
    return lax.dot_general(a, b, NT_DIMS, preferred_element_type=F32)


def _rms(x, g):
    ms = jnp.mean(x * x, axis=-1, keepdims=True)
    return x * lax.rsqrt(ms + EPS) * g


def _rope_pairs(a, g2, cs):
    ms = jnp.sum(a * a, axis=-1, keepdims=True) * (1.0 / LANES)
    t = a * lax.rsqrt(ms + EPS) * g2 * cs
    return t + pltpu.roll(t, D_ROPE, 1)


def _resident(shape):
    nd = len(shape)
    return pl.BlockSpec(shape, lambda *_: (0,) * nd, pipeline_mode=pl.Buffered(1))


def _params(sem):
    return pltpu.CompilerParams(dimension_semantics=sem, vmem_limit_bytes=VMEM_LIMIT_BYTES)


def _gmlp_kernel(n_p_tiles, x_ref, ng_ref, win_ref, bin_ref, sg_ref, ws_ref, bs_ref, wout_ref, bout_ref,
                 xo_ref, v_ref, zu_ref, vb_ref, gated_ref):
    i = pl.program_id(0)
    x = x_ref[...]
    h = _rms(x, ng_ref[...]).astype(BF16)
    dg = sg_ref.shape[1]
    zu_ref[...] = jax.nn.gelu(_dot(h, win_ref[:, :dg]) + bin_ref[:, :dg])
    v = _rms(jax.nn.gelu(_dot(h, win_ref[:, dg:]) + bin_ref[:, dg:]), sg_ref[...])
    vb_ref[...] = v.astype(BF16)

    @pl.when(i >= n_p_tiles)
    def _():
        v_ref[...] = v

    gw = dg // N_GROUPS
    for c in range(x.shape[0] // CHUNK):
        rows = slice(c * CHUNK, (c + 1) * CHUNK)
        for g in range(N_GROUPS):
            cols = slice(g * gw, (g + 1) * gw)
            mix = _dot(ws_ref[0, g], vb_ref[rows, cols]) + bs_ref[0, :, cols]
            gated_ref[rows, cols] = (zu_ref[rows, cols] * mix).astype(BF16)
    xo_ref[...] = x + _dot(gated_ref[...], wout_ref[...]) + bout_ref[...]


def _gmlp_layer(x, n_p_tiles, tm, ng, w_in, b_in, sg, ws_eff, bs_eff, w_out, b_out):
    t, d = x.shape
    dg = sg.shape[1]
    n_tiles = t // tm
    t_s = t - n_p_tiles * tm
    sel = lambda i: jnp.where(i >= n_p_tiles, 1, 0)
    return pl.pallas_call(
        functools.partial(_gmlp_kernel, n_p_tiles),
        grid=(n_tiles,),
        in_specs=[
            pl.BlockSpec((tm, d), lambda i: (i, 0)),
            _resident(ng.shape), _resident(w_in.shape), _resident(b_in.shape), _resident(sg.shape),
            pl.BlockSpec((1,) + ws_eff.shape[1:], lambda i: (sel(i), 0, 0, 0)),
            pl.BlockSpec((1,) + bs_eff.shape[1:], lambda i: (sel(i), 0, 0)),
            _resident(w_out.shape), _resident(b_out.shape),
        ],
        out_specs=[
            pl.BlockSpec((tm, d), lambda i: (i, 0)),
            pl.BlockSpec((tm, dg), lambda i: (jnp.maximum(i - n_p_tiles, 0), 0)),
        ],
        out_shape=[jax.ShapeDtypeStruct((t, d), F32), jax.ShapeDtypeStruct((t_s, dg), F32)],
        scratch_shapes=[pltpu.VMEM((tm, dg), F32), pltpu.VMEM((tm, dg), BF16), pltpu.VMEM((tm, dg), BF16)],
        compiler_params=_params(("arbitrary",)),
        name="gmlp_layer",
    )(x, ng, w_in, b_in, sg, ws_eff, bs_eff, w_out, b_out)


def _ffn_kernel(x_ref, ng_ref, wg_ref, wu_ref, wd_ref, xo_ref):
    x = x_ref[...]
    h = _rms(x, ng_ref[...]).astype(BF16)
    t = (jax.nn.silu(_dot(h, wg_ref[...])) * _dot(h, wu_ref[...])).astype(BF16)
    xo_ref[...] = x + _dot(t, wd_ref[...])


def _ffn_dense(x, tm, ng, wg, wu, wd):
    t, d = x.shape
    return pl.pallas_call(
        _ffn_kernel,
        grid=(t // tm,),
        in_specs=[pl.BlockSpec((tm, d), lambda i: (i, 0)),
                  _resident(ng.shape), _resident(wg.shape), _resident(wu.shape), _resident(wd.shape)],
        out_specs=pl.BlockSpec((tm, d), lambda i: (i, 0)),
        out_shape=jax.ShapeDtypeStruct((t, d), F32),
        compiler_params=_params(("arbitrary",)),
        name="ffn_dense",
    )(x, ng, wg, wu, wd)


MOE_ROW_TILE = 512
DISPATCH_TOKENS = 1024
COMBINE_TOKENS = 256


def _router_kernel(x_ref, ng_ref, wrt_ref, tri_ref, idx_ref, gate_ref, rank_ref, cnt_ref, run_ref):
    @pl.when(pl.program_id(0) == 0)
    def _():
        run_ref[...] = jnp.zeros_like(run_ref)

    hb = _rms(x_ref[...], ng_ref[...]).astype(BF16)
    logits = _dot_nt(wrt_ref[...], hb)
    ne = logits.shape[0]
    ex = jnp.exp(logits - jnp.max(logits, axis=0, keepdims=True))
    probs = ex / jnp.sum(ex, axis=0, keepdims=True)
    row = lax.broadcasted_iota(jnp.int32, probs.shape, 0)
    m1 = jnp.max(probs, axis=0, keepdims=True)
    i1 = jnp.min(jnp.where(probs == m1, row, ne), axis=0, keepdims=True)
    rest = jnp.where(row == i1, -1.0, probs)
    m2 = jnp.max(rest, axis=0, keepdims=True)
    i2 = jnp.min(jnp.where(rest == m2, row, ne), axis=0, keepdims=True)
    den = m1 + m2
    oh1 = row == i1
    oh2 = row == i2
    oh = jnp.where(oh1 | oh2, 1.0, 0.0)
    rank = _dot(oh.astype(BF16), tri_ref[...]) + run_ref[:, :1]
    idx_ref[...] = jnp.concatenate([i1, i2], axis=0)
    gate_ref[...] = jnp.concatenate([m1 / den, m2 / den], axis=0)
    rank_ref[...] = jnp.concatenate([jnp.sum(jnp.where(oh1, rank, 0.0), axis=0, keepdims=True),
                                     jnp.sum(jnp.where(oh2, rank, 0.0), axis=0, keepdims=True)],
                                    axis=0).astype(jnp.int32)
    run = run_ref[...] + jnp.sum(oh, axis=1, keepdims=True)
    run_ref[...] = run
    cnt_ref[...] = run.astype(jnp.int32)


def _router(x, tm, ng, wrt):
    t, d = x.shape
    ne = wrt.shape[0]
    tri = jnp.triu(jnp.ones((tm, tm), BF16), k=1)
    tok = lambda i: (0, i)
    return pl.pallas_call(
        _router_kernel,
        grid=(t // tm,),
        in_specs=[pl.BlockSpec((tm, d), lambda i: (i, 0)), _resident(ng.shape), _resident(wrt.shape),
                  _resident(tri.shape)],
        out_specs=[pl.BlockSpec((TOP_K, tm), tok), pl.BlockSpec((TOP_K, tm), tok), pl.BlockSpec((TOP_K, tm), tok),
                   pl.BlockSpec((ne, LANES), lambda i: (0, 0))],
        out_shape=[jax.ShapeDtypeStruct((TOP_K, t), jnp.int32), jax.ShapeDtypeStruct((TOP_K, t), F32),
                   jax.ShapeDtypeStruct((TOP_K, t), jnp.int32), jax.ShapeDtypeStruct((ne, LANES), jnp.int32)],
        scratch_shapes=[pltpu.VMEM((ne, LANES), F32)],
        compiler_params=_params(("arbitrary",)),
        name="moe_router",
    )(x, ng, wrt, tri)


def _row_copy(src_ref, src_row, dst_ref, dst_row, sem):
    return pltpu.make_async_copy(src_ref.at[pl.ds(src_row, 1)], dst_ref.at[pl.ds(dst_row, 1)], sem)


def _dispatch_kernel(pos_ref, x_ref, xs_in_hbm, xs_hbm, sem):
    del xs_in_hbm
    n = pos_ref.shape[1]

    def issue(r, carry):
        for k in range(TOP_K):
            _row_copy(x_ref, r, xs_hbm, pos_ref[k, r], sem).start()
        return carry

    lax.fori_loop(0, n, issue, 0, unroll=8)
    for k in range(TOP_K):
        pltpu.make_async_copy(x_ref, xs_hbm.at[pl.ds(0, n)], sem).wait()


def _dispatch(x, pos, n_rows):
    t, d = x.shape
    n = DISPATCH_TOKENS if t % DISPATCH_TOKENS == 0 else LANES
    return pl.pallas_call(
        _dispatch_kernel,
        grid=(t // n,),
        in_specs=[pl.BlockSpec((TOP_K, n), lambda i: (0, i), memory_space=pltpu.SMEM),
                  pl.BlockSpec((n, d), lambda i: (i, 0)), pl.BlockSpec(memory_space=pl.ANY)],
        out_specs=pl.BlockSpec(memory_space=pl.ANY),
        out_shape=jax.ShapeDtypeStruct((n_rows, d), F32),
        scratch_shapes=[pltpu.SemaphoreType.DMA(())],
        input_output_aliases={2: 0},
        compiler_params=_params(("arbitrary",)),
        name="moe_dispatch",
    )(pos, x, jnp.zeros((n_rows, d), F32))


def _grouped_ffn_kernel(layer, te_ref, nu_ref, xs_ref, ng_ref, wg_ref, wu_ref, wd_ref, ys_ref):
    del layer, te_ref
    i = pl.program_id(0)

    @pl.when(i < nu_ref[0])
    def _():
        h = _rms(xs_ref[...], ng_ref[...]).astype(BF16)
        t = (jax.nn.silu(_dot(h, wg_ref[0, 0])) * _dot(h, wu_ref[0, 0])).astype(BF16)
        ys_ref[...] = _dot(t, wd_ref[0, 0])

    @pl.when(i >= nu_ref[0])
    def _():
        ys_ref[...] = jnp.zeros_like(ys_ref)


def _grouped_ffn(xs, tile_expert, n_used, layer, ng, wg, wu, wd):
    n_rows, d = xs.shape
    dff = wg.shape[3]
    tmg = MOE_ROW_TILE
    wspec = lambda shape: pl.BlockSpec((1, 1) + shape, lambda i, te, nu: (layer, te[i], 0, 0),
                                       pipeline_mode=pl.Buffered(1))
    return pl.pallas_call(
        functools.partial(_grouped_ffn_kernel, layer),
        grid_spec=pltpu.PrefetchScalarGridSpec(
            num_scalar_prefetch=2,
            grid=(n_rows // tmg,),
            in_specs=[pl.BlockSpec((tmg, d), lambda i, te, nu: (i, 0)),
                      pl.BlockSpec(ng.shape, lambda i, te, nu: (0, 0), pipeline_mode=pl.Buffered(1)),
                      wspec((d, dff)), wspec((d, dff)), wspec((dff, d))],
            out_specs=pl.BlockSpec((tmg, d), lambda i, te, nu: (i, 0))),
        out_shape=jax.ShapeDtypeStruct((n_rows, d), F32),
        compiler_params=_params(("arbitrary",)),
        name="moe_grouped_ffn",
    )(tile_expert, n_used, xs, ng, wg, wu, wd)


def _combine_kernel(pos_ref, x_ref, gate_ref, ys_hbm, xo_ref, buf_ref, sem):
    n = x_ref.shape[0]

    def issue(r, carry):
        for k in range(TOP_K):
            _row_copy(ys_hbm, pos_ref[k, r], buf_ref.at[k], r, sem).start()
        return carry

    lax.fori_loop(0, n, issue, 0, unroll=8)
    for k in range(TOP_K):
        pltpu.make_async_copy(ys_hbm.at[pl.ds(0, n)], buf_ref.at[k], sem).wait()
    g = gate_ref[...]
    xo_ref[...] = x_ref[...] + (g[:, 0:1] * buf_ref[0] + g[:, 1:2] * buf_ref[1])


def _combine(x, pos, gates, ys):
    t, d = x.shape
    n = COMBINE_TOKENS if t % COMBINE_TOKENS == 0 else LANES
    return pl.pallas_call(
        _combine_kernel,
        grid=(t // n,),
        in_specs=[pl.BlockSpec((TOP_K, n), lambda i: (0, i), memory_space=pltpu.SMEM),
                  pl.BlockSpec((n, d), lambda i: (i, 0)), pl.BlockSpec((n, TOP_K), lambda i: (i, 0)),
                  pl.BlockSpec(memory_space=pl.ANY)],
        out_specs=pl.BlockSpec((n, d), lambda i: (i, 0)),
        out_shape=jax.ShapeDtypeStruct((t, d), F32),
        scratch_shapes=[pltpu.VMEM((TOP_K, n, d), F32), pltpu.SemaphoreType.DMA(())],
        compiler_params=_params(("arbitrary",)),
        name="moe_combine",
    )(pos, x, gates, ys)


def _moe_layer(x, tm, layer, ng, wrt, wg, wu, wd):
    t, d = x.shape
    ne = wrt.shape[0]
    tmg = MOE_ROW_TILE
    n_rows = -(-(t * TOP_K) // tmg) * tmg + ne * tmg
    idx, gate, rank, cnt = _router(x, tm, ng, wrt)
    padded = (cnt[:, 0] + tmg - 1) // tmg * tmg
    ends = jnp.cumsum(padded)
    starts = ends - padded
    pos = rank + sum(jnp.where(idx == e, starts[e], 0) for e in range(ne))
    tile_start = jnp.arange(n_rows // tmg, dtype=jnp.int32) * tmg
    tile_expert = jnp.minimum(jnp.sum(tile_start[:, None] >= ends[None, :], axis=1), ne - 1).astype(jnp.int32)
    n_used = (ends[-1:] // tmg).astype(jnp.int32)
    xs = _dispatch(x, pos, n_rows)
    ys = _grouped_ffn(xs, tile_expert, n_used, layer, ng, wg, wu, wd)
    return _combine(x, pos, gate.T, ys)


def _keyside_kernel(x_ref, ng_ref, wdkv_ref, lg_ref, wkr_ref, krg_ref, cs_ref, wukt_ref,
                    c_ref, kr_ref, cb_ref, krp_ref, rt_ref):
    h = _rms(x_ref[...], ng_ref[...]).astype(BF16)
    c = _rms(_dot(h, wdkv_ref[...]), lg_ref[...])
    cb = c.astype(BF16)
    c_ref[...] = c
    cb_ref[...] = cb
    full = _rope_pairs(_dot(h, wkr_ref[...]), krg_ref[...], cs_ref[...])
    kr_ref[...] = full[:, :D_ROPE]
    lane = lax.broadcasted_iota(jnp.int32, full.shape, 1)
    krp_ref[...] = jnp.where(lane < D_ROPE, full, 0.0).astype(BF16)
    rt_ref[...] = _key_rms_scale_t(wukt_ref[...], cb)


def _key_rms_scale_t(wukt, cb):
    kt = _dot_nt(wukt, cb)
    ssq = jnp.sum((kt * kt).reshape(N_HEADS, D_NOPE, kt.shape[1]), axis=1)
    return lax.rsqrt(ssq * (1.0 / D_NOPE) + EPS)


def _key_side(x, tm, cs_map, ng, w_dkv, lg, w_kr2, krg2, cs_tab, wukt):
    t, d = x.shape
    r = w_dkv.shape[1]
    row = lambda i: (i, 0)
    return pl.pallas_call(
        _keyside_kernel,
        grid=(t // tm,),
        in_specs=[pl.BlockSpec((tm, d), row),
                  _resident(ng.shape), _resident(w_dkv.shape), _resident(lg.shape),
                  _resident(w_kr2.shape), _resident(krg2.shape),
                  pl.BlockSpec((tm, LANES), lambda i: (cs_map(i), 0)),
                  _resident(wukt.shape)],
        out_specs=[pl.BlockSpec((tm, r), row), pl.BlockSpec((tm, D_ROPE), row), pl.BlockSpec((tm, r), row),
                   pl.BlockSpec((tm, LANES), row), pl.BlockSpec((N_HEADS, tm), lambda i: (0, i))],
        out_shape=[jax.ShapeDtypeStruct((t, r), F32), jax.ShapeDtypeStruct((t, D_ROPE), F32),
                   jax.ShapeDtypeStruct((t, r), BF16), jax.ShapeDtypeStruct((t, LANES), BF16),
                   jax.ShapeDtypeStruct((N_HEADS, t), F32)],
        compiler_params=_params(("arbitrary",)),
        name="key_side",
    )(x, ng, w_dkv, lg, w_kr2, krg2, cs_tab, wukt)


def _past_scale_kernel(n_pg, pt_ref, *refs):
    wukt_ref = refs[0]
    pages = refs[1:1 + n_pg]
    rt_ref = refs[1 + n_pg]
    cblk_ref = refs[2 + n_pg]
    for g in range(n_pg):
        cblk_ref[g * CHUNK:(g + 1) * CHUNK, :] = pages[g][0].astype(BF16)
    rt_ref[0] = _key_rms_scale_t(wukt_ref[...], cblk_ref[...])


def _page_specs(cache, n_pages, n_pg):
    page, width = cache.shape[1:]
    return [pl.BlockSpec((1, page, width),
                         functools.partial(lambda g, b, j, pt: (pt[b * n_pages + j * n_pg + g], 0, 0), g))
            for g in range(n_pg)]


def _past_scale(page_table, cache_latent, wukt, n_pg):
    n_seq, n_pages = page_table.shape
    page, r = cache_latent.shape[1:]
    gk = n_pg * page
    return pl.pallas_call(
        functools.partial(_past_scale_kernel, n_pg),
        grid_spec=pltpu.PrefetchScalarGridSpec(
            num_scalar_prefetch=1,
            grid=(n_seq, n_pages // n_pg),
            in_specs=[pl.BlockSpec(wukt.shape, lambda b, j, pt: (0, 0), pipeline_mode=pl.Buffered(1))]
            + _page_specs(cache_latent, n_pages, n_pg),
            out_specs=pl.BlockSpec((1, N_HEADS, gk), lambda b, j, pt: (b, 0, j)),
            scratch_shapes=[pltpu.VMEM((gk, r), BF16)]),
        out_shape=jax.ShapeDtypeStruct((n_seq, N_HEADS, n_pages * page), F32),
        compiler_params=_params(("arbitrary", "arbitrary")),
        name="past_key_scale",
    )(page_table.reshape(-1), wukt, *([cache_latent] * n_pg))


def _qside_kernel(x_ref, ng_ref, wdq_ref, qlg_ref, wn_ref, wr_ref, qng_ref, kng_ref, qrg_ref, cs_ref, wukt_ref,
                  qa_ref, qr_ref):
    h = _rms(x_ref[...], ng_ref[...]).astype(BF16)
    cq = _rms(_dot(h, wdq_ref[...]), qlg_ref[...]).astype(BF16)
    qn_all = _dot(cq, wn_ref[...])
    qr_all = _dot(cq, wr_ref[...])
    cs = cs_ref[...]
    for hh in range(N_HEADS):
        cols = slice(hh * LANES, (hh + 1) * LANES)
        qn = (_rms(qn_all[:, cols], qng_ref[...]) * kng_ref[...]).astype(BF16)
        qa_ref[hh] = _dot(qn, wukt_ref[hh]).astype(BF16)
        qr_ref[hh] = _rope_pairs(qr_all[:, cols], qrg_ref[...], cs).astype(BF16)


def _q_side(x, tm, cs_map, ng, w_dq, qlg, wn, wr2, qng, kng, qrg2, cs_tab, wukt3):
    t, d = x.shape
    r = wukt3.shape[2]
    return pl.pallas_call(
        _qside_kernel,
        grid=(t // tm,),
        in_specs=[pl.BlockSpec((tm, d), lambda i: (i, 0)),
                  _resident(ng.shape), _resident(w_dq.shape), _resident(qlg.shape), _resident(wn.shape),
                  _resident(wr2.shape), _resident(qng.shape), _resident(kng.shape), _resident(qrg2.shape),
                  pl.BlockSpec((tm, LANES), lambda i: (cs_map(i), 0)),
                  _resident(wukt3.shape)],
        out_specs=[pl.BlockSpec((N_HEADS, tm, r), lambda i: (0, i, 0)),
                   pl.BlockSpec((N_HEADS, tm, LANES), lambda i: (0, i, 0))],
        out_shape=[jax.ShapeDtypeStruct((N_HEADS, t, r), BF16), jax.ShapeDtypeStruct((N_HEADS, t, LANES), BF16)],
        compiler_params=_params(("arbitrary",)),
        name="q_side",
    )(x, ng, w_dq, qlg, wn, wr2, qng, kng, qrg2, cs_tab, wukt3)


def _softmax_step(s, cb, m, l, acc):
    m_new = jnp.maximum(m, jnp.max(s, axis=1, keepdims=True))
    alpha = jnp.exp(m - m_new)
    p = jnp.exp(s - m_new)
    l_new = alpha * l + jnp.sum(p, axis=1, keepdims=True)
    acc_new = alpha * acc + _dot(p.astype(BF16), cb)
    return m_new, l_new, acc_new


def _scores(s1, s2, rt, scale):
    nrow, nk = s1.shape
    s1 = s1.reshape(N_HEADS, nrow // N_HEADS, nk) * rt[:, None, :]
    return (s1.reshape(nrow, nk) + s2) * scale


def _prompt_attn_kernel(bq, bk, scale, qa_ref, qr_ref, cb_ref, krp_ref, rt_ref, o_ref, m_ref, acc_ref):
    i = pl.program_id(1)
    r = qa_ref.shape[2]
    q = qa_ref[...].reshape(N_HEADS * bq, r)
    qr = qr_ref[...].reshape(N_HEADS * bq, LANES)
    m_ref[...] = jnp.full_like(m_ref, -jnp.inf)
    acc_ref[...] = jnp.zeros_like(acc_ref)
    qpos = i * bq + lax.broadcasted_iota(jnp.int32, (1, bq, bk), 1)
    kidx = lax.broadcasted_iota(jnp.int32, (1, bq, bk), 2)
    ones = jnp.ones((bk, LANES), BF16)

    def step(j, masked):
        k0 = pl.multiple_of(j * bk, bk)
        cb = cb_ref[pl.ds(k0, bk), :]
        rts = rt_ref[:, pl.ds(k0, bk)] * scale
        s = _dot_nt(q, cb).reshape(N_HEADS, bq, bk) * rts[:, None, :]
        s = s + (_dot_nt(qr, krp_ref[pl.ds(k0, bk), :]) * scale).reshape(N_HEADS, bq, bk)
        if masked:
            s = jnp.where(kidx + k0 <= qpos, s, -jnp.inf)
        s = s.reshape(N_HEADS * bq, bk)
        m_prev = m_ref[...]
        m_new = jnp.maximum(m_prev, jnp.max(s, axis=1, keepdims=True))
        alpha = jnp.exp(m_prev - m_new)
        p = jnp.exp(s - jnp.tile(m_new, (1, bk // LANES))).astype(BF16)
        m_ref[...] = m_new
        acc_ref[...] = (jnp.tile(alpha, (1, acc_ref.shape[1] // LANES)) * acc_ref[...]
                        + _dot(p, jnp.concatenate([cb, ones], axis=1)))

    def run(masked):
        def body(j, carry):
            step(j, masked)
            return carry
        return body

    n_visible = (i * bq) // bk
    n_blocks = (i * bq + bq + bk - 1) // bk
    lax.fori_loop(0, n_visible, run(False), 0)
    lax.fori_loop(n_visible, n_blocks, run(True), 0)
    denom = jnp.tile(acc_ref[:, r:], (1, r // LANES))
    o_ref[...] = (acc_ref[:, :r] / denom).reshape(N_HEADS, bq, r).astype(BF16)


def _prompt_attention(qa, qr, cb, krp, rt, bsz, seq, bq, bk, scale):
    r = cb.shape[1]
    nq = seq // bq
    nrow = N_HEADS * bq
    qmap = lambda b, i: (0, b * nq + i, 0)
    return pl.pallas_call(
        functools.partial(_prompt_attn_kernel, bq, bk, scale),
        grid=(bsz, nq),
        in_specs=[pl.BlockSpec((N_HEADS, bq, r), qmap), pl.BlockSpec((N_HEADS, bq, LANES), qmap),
                  pl.BlockSpec((seq, r), lambda b, i: (b, 0)), pl.BlockSpec((seq, LANES), lambda b, i: (b, 0)),
                  pl.BlockSpec((N_HEADS, seq), lambda b, i: (0, b))],
        out_specs=pl.BlockSpec((N_HEADS, bq, r), qmap),
        out_shape=jax.ShapeDtypeStruct((N_HEADS, bsz * seq, r), BF16),
        scratch_shapes=[pltpu.VMEM((nrow, LANES), F32), pltpu.VMEM((nrow, r + LANES), F32)],
        compiler_params=_params(("arbitrary", "arbitrary")),
        name="prompt_attention",
    )(qa, qr, cb, krp, rt)


def _sample_attn_kernel(n_pg, n_new, scale, pt_ref, *refs):
    qa_ref, qr_ref, rtp_ref, cn_ref, krn_ref, rtn_ref = refs[:6]
    cpages = refs[6:6 + n_pg]
    kpages = refs[6 + n_pg:6 + 2 * n_pg]
    o_ref, m_ref, l_ref, acc_ref, cblk_ref, kblk_ref = refs[6 + 2 * n_pg:]
    j = pl.program_id(1)

    @pl.when(j == 0)
    def _():
        m_ref[...] = jnp.full_like(m_ref, -jnp.inf)
        l_ref[...] = jnp.zeros_like(l_ref)
        acc_ref[...] = jnp.zeros_like(acc_ref)

    for g in range(n_pg):
        cblk_ref[g * CHUNK:(g + 1) * CHUNK, :] = cpages[g][0].astype(BF16)
        kblk_ref[:, g * CHUNK:(g + 1) * CHUNK] = kpages[g][0].astype(BF16)
    q = qa_ref[0]
    qr = qr_ref[0][:, :D_ROPE]
    cb = cblk_ref[...]
    s = _scores(_dot_nt(q, cb), _dot(qr, kblk_ref[...]), rtp_ref[0], scale)
    m, l, acc = _softmax_step(s, cb, m_ref[...], l_ref[...], acc_ref[...])
    m_ref[...] = m
    l_ref[...] = l
    acc_ref[...] = acc

    @pl.when(j == pl.num_programs(1) - 1)
    def _():
        pad = CHUNK - n_new
        cn = jnp.concatenate([cn_ref[0], jnp.zeros((pad, cn_ref.shape[2]), F32)], axis=0).astype(BF16)
        krn = jnp.concatenate([krn_ref[0], jnp.zeros((pad, D_ROPE), F32)], axis=0).astype(BF16)
        sn = _scores(_dot_nt(q, cn), _dot_nt(qr, krn), rtn_ref[0], scale)
        nrow = q.shape[0]
        qi = lax.broadcasted_iota(jnp.int32, (N_HEADS, nrow // N_HEADS, CHUNK), 1).reshape(nrow, CHUNK)
        ki = lax.broadcasted_iota(jnp.int32, (nrow, CHUNK), 1)
        sn = jnp.where(ki <= qi, sn, -jnp.inf)
        _, l2, acc2 = _softmax_step(sn, cn, m, l, acc)
        o_ref[0] = (acc2 / l2).astype(BF16)


def _sample_attention(page_table, qa_s, qr_s, rt_past, c_new, kr_new, rt_new, cache_latent, cache_krope_t,
                      n_pg, scale):
    n_seq, n_pages = page_table.shape
    page, r = cache_latent.shape[1:]
    nrow = qa_s.shape[1]
    n_new = c_new.shape[1]
    gk = n_pg * page
    seq_blk = lambda shape: pl.BlockSpec((1,) + shape, lambda b, j, pt: (b, 0, 0))
    return pl.pallas_call(
        functools.partial(_sample_attn_kernel, n_pg, n_new, scale),
        grid_spec=pltpu.PrefetchScalarGridSpec(
            num_scalar_prefetch=1,
            grid=(n_seq, n_pages // n_pg),
            in_specs=[seq_blk((nrow, r)), seq_blk((nrow, LANES)),
                      pl.BlockSpec((1, N_HEADS, gk), lambda b, j, pt: (b, 0, j)),
                      seq_blk((n_new, r)), seq_blk((n_new, D_ROPE)), seq_blk((N_HEADS, LANES))]
            + _page_specs(cache_latent, n_pages, n_pg) + _page_specs(cache_krope_t, n_pages, n_pg),
            out_specs=seq_blk((nrow, r)),
            scratch_shapes=[pltpu.VMEM((nrow, 1), F32), pltpu.VMEM((nrow, 1), F32), pltpu.VMEM((nrow, r), F32),
                            pltpu.VMEM((gk, r), BF16), pltpu.VMEM((D_ROPE, gk), BF16)]),
        out_shape=jax.ShapeDtypeStruct((n_seq, nrow, r), BF16),
        compiler_params=_params(("arbitrary", "arbitrary")),
        name="sample_attention",
    )(page_table.reshape(-1), qa_s, qr_s, rt_past, c_new, kr_new, rt_new,
      *([cache_latent] * n_pg), *([cache_krope_t] * n_pg))


def _attn_out_kernel(n_p_tiles, x_ref, op_ref, os_ref, wuv_ref, wo_ref, xo_ref):
    i = pl.program_id(0)

    def project(o_ref):
        o = jnp.concatenate([_dot(o_ref[hh], wuv_ref[hh]) for hh in range(N_HEADS)], axis=1).astype(BF16)
        xo_ref[...] = x_ref[...] + _dot(o, wo_ref[...])

    @pl.when(i < n_p_tiles)
    def _():
        project(op_ref)

    @pl.when(i >= n_p_tiles)
    def _():
        project(os_ref)


def _attn_out(x, o_p, o_s, n_p_tiles, tm, w_uv3, w_o):
    t, d = x.shape
    r = o_p.shape[2]
    return pl.pallas_call(
        functools.partial(_attn_out_kernel, n_p_tiles),
        grid=(t // tm,),
        in_specs=[pl.BlockSpec((tm, d), lambda i: (i, 0)),
                  pl.BlockSpec((N_HEADS, tm, r), lambda i: (0, jnp.minimum(i, n_p_tiles - 1), 0)),
                  pl.BlockSpec((N_HEADS, tm, r), lambda i: (0, jnp.maximum(i - n_p_tiles, 0), 0)),
                  _resident(w_uv3.shape), _resident(w_o.shape)],
        out_specs=pl.BlockSpec((tm, d), lambda i: (i, 0)),
        out_shape=jax.ShapeDtypeStruct((t, d), F32),
        compiler_params=_params(("arbitrary",)),
        name="attn_out",
    )(x, o_p, o_s, w_uv3, w_o)


def _swap_halves(a):
    half = a.shape[-1] // 2
    return jnp.concatenate([a[..., half:], a[..., :half]], axis=-1)


def _rope_table(pos):
    half = D_ROPE // 2
    inv = ROPE_THETA ** (-jnp.arange(half, dtype=F32) / half)
    ang = pos.astype(F32)[:, None] * inv[None, :]
    cos, sin = jnp.cos(ang), jnp.sin(ang)
    return jnp.concatenate([cos, cos, -sin, sin], axis=1)


def _spatial_weights(w_s, b_s, length, gw):
    reps = CHUNK // length
    w = w_s[:, :length, :length] * jnp.tril(jnp.ones((length, length), w_s.dtype))
    w_eff = jnp.einsum("st,gpq->gsptq", jnp.eye(reps, dtype=w.dtype), w).reshape(N_GROUPS, CHUNK, CHUNK)
    b_eff = jnp.repeat(jnp.tile(b_s[:, :length].T, (reps, 1)), gw, axis=1)
    return w_eff, b_eff


def kernel(x_prompt, x_sample, cache_latent, cache_krope, page_table, a_norm_g, a_w_in, a_b_in, a_sgu_g, a_w_s, a_b_s, a_w_out, a_b_out, kv_norm_g, w_dkv, kv_lat_g, w_kr, kr_g, kn_g, w_uk, w_uv, b_norm_g, b_w_dq, b_q_lat_g, b_w_uq, b_qn_g, b_qr_g, b_w_o, f_norm_g, d_w_gate, d_w_up, d_w_down, m_w_router, m_w_gate, m_w_up, m_w_down):
    bsz, seq, d = x_prompt.shape
    n_seq, dec = x_sample.shape[:2]
    t_p, t_s = bsz * seq, n_seq * dec
    n_pages = page_table.shape[1]
    past_len = n_pages * cache_latent.shape[1]
    depth = f_norm_g.shape[0]
    n_a = a_w_in.shape[0]
    dg = a_sgu_g.shape[1]
    kv_rank = w_dkv.shape[1]
    scale = float((D_NOPE + D_ROPE) ** -0.5)

    tm = 512 if t_s % 512 == 0 else CHUNK
    assert t_p % tm == 0 and t_s % tm == 0 and seq % tm == 0 and tm % CHUNK == 0
    assert seq % CHUNK == 0 and CHUNK % dec == 0 and dec <= CHUNK and tm % dec == 0
    n_p_tiles = t_p // tm
    n_pg_attn = next(g for g in (32, 16, 8, 4, 2, 1) if n_pages % g == 0)
    n_pg_scale = next(g for g in (16, 8, 4, 2, 1) if n_pages % g == 0)
    kv_block = next(k for k in (512, 256, CHUNK) if seq % k == 0)
    row = lambda a: a.reshape(1, -1)
    bf = lambda a: a.astype(BF16)

    x = jnp.concatenate([x_prompt.reshape(t_p, d), x_sample.reshape(t_s, d)], axis=0)

    pos_p = jnp.arange(seq, dtype=jnp.int32)
    pos_s = past_len + jnp.arange(dec, dtype=jnp.int32)
    cs_tab = jnp.concatenate([_rope_table(pos_p), jnp.tile(_rope_table(pos_s), (tm // dec, 1))], axis=0)
    tiles_per_seq = seq // tm
    cs_map = lambda i: jnp.where(i < n_p_tiles, i % tiles_per_seq, tiles_per_seq)

    cache_krope_t = jnp.swapaxes(cache_krope, 1, 2)
    wukt3 = bf(jnp.transpose(w_uk, (1, 2, 0)))
    wukt = wukt3.reshape(N_HEADS * D_NOPE, kv_rank)
    w_uv3 = bf(jnp.transpose(w_uv, (1, 0, 2)))
    m_wg, m_wu, m_wd = bf(m_w_gate), bf(m_w_up), bf(m_w_down)

    v_rows = []
    c = kr = cb = krp = rt = rt_past = None
    for layer in range(depth):
        if layer < n_a:
            a = layer
            ws_p, bs_p = _spatial_weights(a_w_s[a], a_b_s[a], min(seq, CHUNK), dg // N_GROUPS)
            ws_s, bs_s = _spatial_weights(a_w_s[a], a_b_s[a], min(dec, CHUNK), dg // N_GROUPS)
            x, v = _gmlp_layer(x, n_p_tiles, tm, row(a_norm_g[a]), bf(a_w_in[a]), row(a_b_in[a]), row(a_sgu_g[a]),
                               bf(jnp.stack([ws_p, ws_s])), jnp.stack([bs_p, bs_s]), bf(a_w_out[a]), row(a_b_out[a]))
            v_rows.append(v.reshape(n_seq, dec, dg))
        else:
            if layer == n_a:
                w_kr2 = bf(jnp.concatenate([w_kr, _swap_halves(w_kr)], axis=1))
                krg2 = row(jnp.concatenate([kr_g, _swap_halves(kr_g)]))
                c, kr, cb, krp, rt = _key_side(x, tm, cs_map, row(kv_norm_g), bf(w_dkv), row(kv_lat_g), w_kr2, krg2,
                                               cs_tab, wukt)
                rt_past = _past_scale(page_table, cache_latent, wukt, n_pg_scale)
                c_new = c[t_p:].reshape(n_seq, dec, kv_rank)
                kr_new = kr[t_p:].reshape(n_seq, dec, D_ROPE)
                rt_new = jnp.transpose(rt[:, t_p:].reshape(N_HEADS, n_seq, dec), (1, 0, 2))
                rt_new = jnp.pad(rt_new, ((0, 0), (0, 0), (0, LANES - dec)))
            b = layer - n_a
            w_uq = b_w_uq[b]
            wn = bf(w_uq[:, :, :D_NOPE].reshape(w_uq.shape[0], N_HEADS * D_NOPE))
            wr = w_uq[:, :, D_NOPE:]
            wr2 = bf(jnp.concatenate([wr, _swap_halves(wr)], axis=2).reshape(w_uq.shape[0], N_HEADS * LANES))
            qrg2 = row(jnp.concatenate([b_qr_g[b], _swap_halves(b_qr_g[b])]))
            qa, qr = _q_side(x, tm, cs_map, row(b_norm_g[b]), bf(b_w_dq[b]), row(b_q_lat_g[b]), wn, wr2,
                             row(b_qn_g[b]), row(kn_g), qrg2, cs_tab, wukt3)
            o_p = _prompt_attention(qa, qr, cb, krp, rt, bsz, seq, QBLOCK, kv_block, scale)

            def per_seq(a):
                w = a.shape[2]
                return jnp.transpose(a[:, t_p:].reshape(N_HEADS, n_seq, dec, w), (1, 0, 2, 3)).reshape(
                    n_seq, N_HEADS * dec, w)

            o_s = _sample_attention(page_table, per_seq(qa), per_seq(qr), rt_past, c_new, kr_new, rt_new,
                                    cache_latent, cache_krope_t, n_pg_attn, scale)
            o_s = jnp.transpose(o_s.reshape(n_seq, N_HEADS, dec, kv_rank), (1, 0, 2, 3)).reshape(
                N_HEADS, t_s, kv_rank)
            x = _attn_out(x, o_p, o_s, n_p_tiles, tm, w_uv3, bf(b_w_o[b]))
        i = layer // 2
        if layer % 2 == 0:
            x = _ffn_dense(x, tm, row(f_norm_g[layer]), bf(d_w_gate[i]), bf(d_w_up[i]), bf(d_w_down[i]))
        else:
            x = _moe_layer(x, tm, i, row(f_norm_g[layer]), bf(m_w_router[i].T), m_wg, m_wu, m_wd)

    return (x[:t_p].reshape(bsz, seq, d), x[t_p:].reshape(n_seq, dec, d),
            c[:t_p].reshape(bsz, seq, kv_rank), kr[:t_p].reshape(bsz, seq, D_ROPE),
            c[t_p:].reshape(n_seq, dec, kv_rank), kr[t_p:].reshape(n_seq, dec, D_ROPE),
            jnp.stack(v_rows))
```

```python
import functools

import jax
import jax.numpy as jnp
from jax import lax
from jax.experimental import pallas as pl
from jax.experimental.pallas import tpu as pltpu

EPS = 1e-6
ROPE_THETA = 10000.0
CHUNK = 128
N_GROUPS = 8
N_HEADS = 8
D_NOPE = 128
D_ROPE = 64
D_V = 128
TOP_K = 2
QBLOCK = 128
LANES = 128
VMEM_LIMIT_BYTES = 60 * 1024 * 1024

BF16 = jnp.bfloat16
F32 = jnp.float32
NT_DIMS = (((1,), (1,)), ((), ()))


def _dot(a, b):
    return jnp.dot(a, b, preferred_element_type=F32)


def _dot_nt(a, b):
    return lax.dot_general(a, b, NT_DIMS, preferred_element_type=F32)


def _rms(x, g):
    ms = jnp.mean(x * x, axis=-1, keepdims=True)
    return x * lax.rsqrt(ms + EPS) * g


def _rope_pairs(a, g2, cs):
    ms = jnp.sum(a * a, axis=-1, keepdims=True) * (1.0 / LANES)
    t = a * lax.rsqrt(ms + EPS) * g2 * cs
    return t + pltpu.roll(t, D_ROPE, 1)


def _resident(shape):
    nd = len(shape)
    return pl.BlockSpec(shape, lambda *_: (0,) * nd, pipeline_mode=pl.Buffered(1))


def _params(sem):
    return pltpu.CompilerParams(dimension_semantics=sem, vmem_limit_bytes=VMEM_LIMIT_BYTES)


def _gmlp_kernel(n_p_tiles, n_x, *refs):
    x_refs = refs[:n_x]
    ng_ref, win_ref, bin_ref, sg_ref, ws_ref, bs_ref, wout_ref, bout_ref = refs[n_x:n_x + 8]
    xo_ref, v_ref, zu_ref, vb_ref, gated_ref = refs[n_x + 8:]
    i = pl.program_id(0)
    x = x_refs[0][...] if n_x == 1 else jnp.where(i < n_p_tiles, x_refs[0][...], x_refs[1][...])
    h = _rms(x, ng_ref[...]).astype(BF16)
    dg = sg_ref.shape[1]
    zu_ref[...] = jax.nn.gelu(_dot(h, win_ref[:, :dg]) + bin_ref[:, :dg])
    v = _rms(jax.nn.gelu(_dot(h, win_ref[:, dg:]) + bin_ref[:, dg:]), sg_ref[...])
    vb_ref[...] = v.astype(BF16)

    @pl.when(i >= n_p_tiles)
    def _():
        v_ref[...] = v

    gw = dg // N_GROUPS
    for c in range(x.shape[0] // CHUNK):
        rows = slice(c * CHUNK, (c + 1) * CHUNK)
        for g in range(N_GROUPS):
            cols = slice(g * gw, (g + 1) * gw)
            mix = _dot(ws_ref[0, g], vb_ref[rows, cols]) + bs_ref[0, :, cols]
            gated_ref[rows, cols] = (zu_ref[rows, cols] * mix).astype(BF16)
    xo_ref[...] = x + _dot(gated_ref[...], wout_ref[...]) + bout_ref[...]


def _gmlp_layer(xs, n_p_tiles, tm, ng, w_in, b_in, sg, ws_eff, bs_eff, w_out, b_out):
    d = xs[0].shape[1]
    t = sum(a.shape[0] for a in xs)
    dg = sg.shape[1]
    n_tiles = t // tm
    t_s = t - n_p_tiles * tm
    sel = lambda i: jnp.where(i >= n_p_tiles, 1, 0)
    if len(xs) == 1:
        x_specs = [pl.BlockSpec((tm, d), lambda i: (i, 0))]
    else:
        x_specs = [pl.BlockSpec((tm, d), lambda i: (jnp.minimum(i, n_p_tiles - 1), 0)),
                   pl.BlockSpec((tm, d), lambda i: (jnp.maximum(i - n_p_tiles, 0), 0))]
    return pl.pallas_call(
        functools.partial(_gmlp_kernel, n_p_tiles, len(xs)),
        grid=(n_tiles,),
        in_specs=x_specs + [
            _resident(ng.shape), _resident(w_in.shape), _resident(b_in.shape), _resident(sg.shape),
            pl.BlockSpec((1,) + ws_eff.shape[1:], lambda i: (sel(i), 0, 0, 0)),
            pl.BlockSpec((1,) + bs_eff.shape[1:], lambda i: (sel(i), 0, 0)),
            _resident(w_out.shape), _resident(b_out.shape),
        ],
        out_specs=[
            pl.BlockSpec((tm, d), lambda i: (i, 0)),
            pl.BlockSpec((tm, dg), lambda i: (jnp.maximum(i - n_p_tiles, 0), 0)),
        ],
        out_shape=[jax.ShapeDtypeStruct((t, d), F32), jax.ShapeDtypeStruct((t_s, dg), F32)],
        scratch_shapes=[pltpu.VMEM((tm, dg), F32), pltpu.VMEM((tm, dg), BF16), pltpu.VMEM((tm, dg), BF16)],
        compiler_params=_params(("arbitrary",)),
        name="gmlp_layer",
    )(*xs, ng, w_in, b_in, sg, ws_eff, bs_eff, w_out, b_out)


def _ffn_kernel(x_ref, ng_ref, wg_ref, wu_ref, wd_ref, xo_ref):
    x = x_ref[...]
    h = _rms(x, ng_ref[...]).astype(BF16)
    t = (jax.nn.silu(_dot(h, wg_ref[...])) * _dot(h, wu_ref[...])).astype(BF16)
    xo_ref[...] = x + _dot(t, wd_ref[...])


def _ffn_dense(x, tm, ng, wg, wu, wd):
    t, d = x.shape
    return pl.pallas_call(
        _ffn_kernel,
        grid=(t // tm,),
        in_specs=[pl.BlockSpec((tm, d), lambda i: (i, 0)),
                  _resident(ng.shape), _resident(wg.shape), _resident(wu.shape), _resident(wd.shape)],
        out_specs=pl.BlockSpec((tm, d), lambda i: (i, 0)),
        out_shape=jax.ShapeDtypeStruct((t, d), F32),
        compiler_params=_params(("arbitrary",)),
        name="ffn_dense",
    )(x, ng, wg, wu, wd)


MOE_ROW_TILE = 512
DISPATCH_TOKENS = 1024
COMBINE_TOKENS = 256


def _router_kernel(x_ref, ng_ref, wrt_ref, tri_ref, idx_ref, gate_ref, rank_ref, cnt_ref, run_ref):
    @pl.when(pl.program_id(0) == 0)
    def _():
        run_ref[...] = jnp.zeros_like(run_ref)

    hb = _rms(x_ref[...], ng_ref[...]).astype(BF16)
    logits = _dot_nt(wrt_ref[...], hb)
    ne = logits.shape[0]
    ex = jnp.exp(logits - jnp.max(logits, axis=0, keepdims=True))
    probs = ex / jnp.sum(ex, axis=0, keepdims=True)
    row = lax.broadcasted_iota(jnp.int32, probs.shape, 0)
    m1 = jnp.max(probs, axis=0, keepdims=True)
    i1 = jnp.min(jnp.where(probs == m1, row, ne), axis=0, keepdims=True)
    rest = jnp.where(row == i1, -1.0, probs)
    m2 = jnp.max(rest, axis=0, keepdims=True)
    i2 = jnp.min(jnp.where(rest == m2, row, ne), axis=0, keepdims=True)
    den = m1 + m2
    oh1 = row == i1
    oh2 = row == i2
    oh = jnp.where(oh1 | oh2, 1.0, 0.0)
    rank = _dot(oh.astype(BF16), tri_ref[...]) + run_ref[:, :1]
    idx_ref[...] = jnp.concatenate([i1, i2], axis=0)
    gate_ref[...] = jnp.concatenate([m1 / den, m2 / den], axis=0)
    rank_ref[...] = jnp.concatenate([jnp.sum(jnp.where(oh1, rank, 0.0), axis=0, keepdims=True),
                                     jnp.sum(jnp.where(oh2, rank, 0.0), axis=0, keepdims=True)],
                                    axis=0).astype(jnp.int32)
    run = run_ref[...] + jnp.sum(oh, axis=1, keepdims=True)
    run_ref[...] = run
    cnt_ref[...] = run.astype(jnp.int32)


def _router(x, tm, ng, wrt):
    t, d = x.shape
    ne = wrt.shape[0]
    tri = jnp.triu(jnp.ones((tm, tm), BF16), k=1)
    tok = lambda i: (0, i)
    return pl.pallas_call(
        _router_kernel,
        grid=(t // tm,),
        in_specs=[pl.BlockSpec((tm, d), lambda i: (i, 0)), _resident(ng.shape), _resident(wrt.shape),
                  _resident(tri.shape)],
        out_specs=[pl.BlockSpec((TOP_K, tm), tok), pl.BlockSpec((TOP_K, tm), tok), pl.BlockSpec((TOP_K, tm), tok),
                   pl.BlockSpec((ne, LANES), lambda i: (0, 0))],
        out_shape=[jax.ShapeDtypeStruct((TOP_K, t), jnp.int32), jax.ShapeDtypeStruct((TOP_K, t), F32),
                   jax.ShapeDtypeStruct((TOP_K, t), jnp.int32), jax.ShapeDtypeStruct((ne, LANES), jnp.int32)],
        scratch_shapes=[pltpu.VMEM((ne, LANES), F32)],
        compiler_params=_params(("arbitrary",)),
        name="moe_router",
    )(x, ng, wrt, tri)


def _row_copy(src_ref, src_row, dst_ref, dst_row, sem):
    return pltpu.make_async_copy(src_ref.at[pl.ds(src_row, 1)], dst_ref.at[pl.ds(dst_row, 1)], sem)


def _dispatch_kernel(fill_lo_ref, fill_hi_ref, pos_ref, x_ref, xs_hbm, zero_ref, sem, zsem):
    n = pos_ref.shape[1]

    @pl.when(pl.program_id(0) == 0)
    def _():
        zero_ref[...] = jnp.zeros_like(zero_ref)
        for e in range(fill_lo_ref.shape[0]):
            lo, hi = fill_lo_ref[e], fill_hi_ref[e]

            def fill(row, carry):
                _row_copy(zero_ref, 0, xs_hbm, row, zsem).start()
                return carry

            def drain(row, carry):
                _row_copy(zero_ref, 0, xs_hbm, row, zsem).wait()
                return carry

            lax.fori_loop(lo, hi, fill, 0)
            lax.fori_loop(lo, hi, drain, 0)

    def issue(r, carry):
        for k in range(TOP_K):
            _row_copy(x_ref, r, xs_hbm, pos_ref[k, r], sem).start()
        return carry

    lax.fori_loop(0, n, issue, 0, unroll=8)
    for k in range(TOP_K):
        pltpu.make_async_copy(x_ref, xs_hbm.at[pl.ds(0, n)], sem).wait()


def _dispatch(x, pos, fill_lo, fill_hi, n_rows):
    t, d = x.shape
    n = DISPATCH_TOKENS if t % DISPATCH_TOKENS == 0 else LANES
    return pl.pallas_call(
        _dispatch_kernel,
        grid_spec=pltpu.PrefetchScalarGridSpec(
            num_scalar_prefetch=2,
            grid=(t // n,),
            in_specs=[pl.BlockSpec((TOP_K, n), lambda i, lo, hi: (0, i), memory_space=pltpu.SMEM),
                      pl.BlockSpec((n, d), lambda i, lo, hi: (i, 0))],
            out_specs=pl.BlockSpec(memory_space=pl.ANY),
            scratch_shapes=[pltpu.VMEM((8, d), F32), pltpu.SemaphoreType.DMA(()), pltpu.SemaphoreType.DMA(())]),
        out_shape=jax.ShapeDtypeStruct((n_rows, d), F32),
        compiler_params=_params(("arbitrary",)),
        name="moe_dispatch",
    )(fill_lo, fill_hi, pos, x)


def _grouped_ffn_kernel(layer, te_ref, nu_ref, xs_ref, ng_ref, wg_ref, wu_ref, wd_ref, ys_ref):
    del layer, te_ref
    i = pl.program_id(0)

    @pl.when(i < nu_ref[0])
    def _():
        h = _rms(xs_ref[...], ng_ref[...]).astype(BF16)
        t = (jax.nn.silu(_dot(h, wg_ref[0, 0])) * _dot(h, wu_ref[0, 0])).astype(BF16)
        ys_ref[...] = _dot(t, wd_ref[0, 0])

    @pl.when(i >= nu_ref[0])
    def _():
        ys_ref[...] = jnp.zeros_like(ys_ref)


def _grouped_ffn(xs, tile_expert, n_used, layer, ng, wg, wu, wd):
    n_rows, d = xs.shape
    dff = wg.shape[3]
    tmg = MOE_ROW_TILE
    wspec = lambda shape: pl.BlockSpec((1, 1) + shape, lambda i, te, nu: (layer, te[i], 0, 0),
                                       pipeline_mode=pl.Buffered(1))
    return pl.pallas_call(
        functools.partial(_grouped_ffn_kernel, layer),
        grid_spec=pltpu.PrefetchScalarGridSpec(
            num_scalar_prefetch=2,
            grid=(n_rows // tmg,),
            in_specs=[pl.BlockSpec((tmg, d), lambda i, te, nu: (jnp.minimum(i, jnp.maximum(nu[0], 1) - 1), 0)),
                      pl.BlockSpec(ng.shape, lambda i, te, nu: (0, 0), pipeline_mode=pl.Buffered(1)),
                      wspec((d, dff)), wspec((d, dff)), wspec((dff, d))],
            out_specs=pl.BlockSpec((tmg, d), lambda i, te, nu: (i, 0))),
        out_shape=jax.ShapeDtypeStruct((n_rows, d), F32),
        compiler_params=_params(("arbitrary",)),
        name="moe_grouped_ffn",
    )(tile_expert, n_used, xs, ng, wg, wu, wd)


def _combine_kernel(n_first, pos_ref, pos_next_ref, x_ref, gate_ref, ys_hbm, *refs):
    outs, (buf_ref, sem) = refs[:-2], refs[-2:]
    n = x_ref.shape[0]
    i = pl.program_id(0)
    slot = i % 2

    def gather(p_ref, s):
        def issue(r, carry):
            for k in range(TOP_K):
                _row_copy(ys_hbm, p_ref[k, r], buf_ref.at[s, k], r, sem.at[s]).start()
            return carry
        lax.fori_loop(0, n, issue, 0, unroll=8)

    @pl.when(i == 0)
    def _():
        gather(pos_ref, 0)

    @pl.when(i + 1 < pl.num_programs(0))
    def _():
        gather(pos_next_ref, 1 - slot)

    for k in range(TOP_K):
        pltpu.make_async_copy(ys_hbm.at[pl.ds(0, n)], buf_ref.at[slot, k], sem.at[slot]).wait()
    g = gate_ref[...]
    y = x_ref[...] + (g[:, 0:1] * buf_ref[slot, 0] + g[:, 1:2] * buf_ref[slot, 1])
    if len(outs) == 1:
        outs[0][...] = y
    else:
        @pl.when(i < n_first)
        def _():
            outs[0][...] = y

        @pl.when(i >= n_first)
        def _():
            outs[1][...] = y


def _combine(x, pos, gates, ys, t_first=None):
    t, d = x.shape
    n = COMBINE_TOKENS if t % COMBINE_TOKENS == 0 else LANES
    n_steps = t // n
    if t_first is None:
        n_first = n_steps
        out_specs = pl.BlockSpec((n, d), lambda i: (i, 0))
        out_shape = jax.ShapeDtypeStruct((t, d), F32)
    else:
        assert t_first % n == 0 and 0 < t_first < t
        n_first = t_first // n
        out_specs = [pl.BlockSpec((n, d), lambda i: (jnp.minimum(i, n_first - 1), 0)),
                     pl.BlockSpec((n, d), lambda i: (jnp.maximum(i - n_first, 0), 0))]
        out_shape = [jax.ShapeDtypeStruct((t_first, d), F32), jax.ShapeDtypeStruct((t - t_first, d), F32)]
    return pl.pallas_call(
        functools.partial(_combine_kernel, n_first),
        grid=(n_steps,),
        in_specs=[pl.BlockSpec((TOP_K, n), lambda i: (0, i), memory_space=pltpu.SMEM),
                  pl.BlockSpec((TOP_K, n), lambda i: (0, jnp.minimum(i + 1, n_steps - 1)), memory_space=pltpu.SMEM),
                  pl.BlockSpec((n, d), lambda i: (i, 0)), pl.BlockSpec((n, TOP_K), lambda i: (i, 0)),
                  pl.BlockSpec(memory_space=pl.ANY)],
        out_specs=out_specs,
        out_shape=out_shape,
        scratch_shapes=[pltpu.VMEM((2, TOP_K, n, d), F32), pltpu.SemaphoreType.DMA((2,))],
        compiler_params=_params(("arbitrary",)),
        name="moe_combine",
    )(pos, pos, x, gates, ys)


def _moe_layer(x, tm, layer, ng, wrt, wg, wu, wd, t_first=None):
    t, d = x.shape
    ne = wrt.shape[0]
    tmg = MOE_ROW_TILE
    n_rows = -(-(t * TOP_K) // tmg) * tmg + ne * tmg
    idx, gate, rank, cnt = _router(x, tm, ng, wrt)
    padded = (cnt[:, 0] + tmg - 1) // tmg * tmg
    ends = jnp.cumsum(padded)
    starts = ends - padded
    pos = rank + sum(jnp.where(idx == e, starts[e], 0) for e in range(ne))
    tile_start = jnp.arange(n_rows // tmg, dtype=jnp.int32) * tmg
    tile_expert = jnp.minimum(jnp.sum(tile_start[:, None] >= ends[None, :], axis=1), ne - 1).astype(jnp.int32)
    n_used = (ends[-1:] // tmg).astype(jnp.int32)
    fill_hi = jnp.where(jnp.arange(ne) == ne - 1, n_rows, ends).astype(jnp.int32)
    xs = _dispatch(x, pos, starts + cnt[:, 0], fill_hi, n_rows)
    ys = _grouped_ffn(xs, tile_expert, n_used, layer, ng, wg, wu, wd)
    return _combine(x, pos, gate.T, ys, t_first)


def _keyside_kernel(x_ref, ng_ref, wdkv_ref, lg_ref, wkr_ref, krg_ref, cs_ref, wukt_ref,
                    c_ref, kr_ref, cb_ref, krp_ref, rt_ref):
    h = _rms(x_ref[...], ng_ref[...]).astype(BF16)
    c = _rms(_dot(h, wdkv_ref[...]), lg_ref[...])
    cb = c.astype(BF16)
    c_ref[...] = c
    cb_ref[...] = cb
    full = _rope_pairs(_dot(h, wkr_ref[...]), krg_ref[...], cs_ref[...])
    kr_ref[...] = full[:, :D_ROPE]
    lane = lax.broadcasted_iota(jnp.int32, full.shape, 1)
    krp_ref[...] = jnp.where(lane < D_ROPE, full, 0.0).astype(BF16)
    rt_ref[...] = _key_rms_scale_t(wukt_ref[...], cb)


def _key_rms_scale_t(wukt, cb):
    kt = _dot_nt(wukt, cb)
    ssq = jnp.sum((kt * kt).reshape(N_HEADS, D_NOPE, kt.shape[1]), axis=1)
    return lax.rsqrt(ssq * (1.0 / D_NOPE) + EPS)


def _key_side(x, tm, cs_map, ng, w_dkv, lg, w_kr2, krg2, cs_tab, wukt):
    t, d = x.shape
    r = w_dkv.shape[1]
    row = lambda i: (i, 0)
    return pl.pallas_call(
        _keyside_kernel,
        grid=(t // tm,),
        in_specs=[pl.BlockSpec((tm, d), row),
                  _resident(ng.shape), _resident(w_dkv.shape), _resident(lg.shape),
                  _resident(w_kr2.shape), _resident(krg2.shape),
                  pl.BlockSpec((tm, LANES), lambda i: (cs_map(i), 0)),
                  _resident(wukt.shape)],
        out_specs=[pl.BlockSpec((tm, r), row), pl.BlockSpec((tm, D_ROPE), row), pl.BlockSpec((tm, r), row),
                   pl.BlockSpec((tm, LANES), row), pl.BlockSpec((N_HEADS, tm), lambda i: (0, i))],
        out_shape=[jax.ShapeDtypeStruct((t, r), F32), jax.ShapeDtypeStruct((t, D_ROPE), F32),
                   jax.ShapeDtypeStruct((t, r), BF16), jax.ShapeDtypeStruct((t, LANES), BF16),
                   jax.ShapeDtypeStruct((N_HEADS, t), F32)],
        compiler_params=_params(("arbitrary",)),
        name="key_side",
    )(x, ng, w_dkv, lg, w_kr2, krg2, cs_tab, wukt)


def _past_scale_kernel(n_pg, pt_ref, *refs):
    wukt_ref = refs[0]
    pages = refs[1:1 + n_pg]
    rt_ref = refs[1 + n_pg]
    cblk_ref = refs[2 + n_pg]
    for g in range(n_pg):
        cblk_ref[g * CHUNK:(g + 1) * CHUNK, :] = pages[g][0].astype(BF16)
    rt_ref[0] = _key_rms_scale_t(wukt_ref[...], cblk_ref[...])


def _page_specs(cache, n_pages, n_pg):
    page, width = cache.shape[1:]
    return [pl.BlockSpec((1, page, width),
                         functools.partial(lambda g, b, j, pt: (pt[b * n_pages + j * n_pg + g], 0, 0), g))
            for g in range(n_pg)]


def _past_scale(page_table, cache_latent, wukt, n_pg):
    n_seq, n_pages = page_table.shape
    page, r = cache_latent.shape[1:]
    gk = n_pg * page
    return pl.pallas_call(
        functools.partial(_past_scale_kernel, n_pg),
        grid_spec=pltpu.PrefetchScalarGridSpec(
            num_scalar_prefetch=1,
            grid=(n_seq, n_pages // n_pg),
            in_specs=[pl.BlockSpec(wukt.shape, lambda b, j, pt: (0, 0), pipeline_mode=pl.Buffered(1))]
            + _page_specs(cache_latent, n_pages, n_pg),
            out_specs=pl.BlockSpec((1, N_HEADS, gk), lambda b, j, pt: (b, 0, j)),
            scratch_shapes=[pltpu.VMEM((gk, r), BF16)]),
        out_shape=jax.ShapeDtypeStruct((n_seq, N_HEADS, n_pages * page), F32),
        compiler_params=_params(("arbitrary", "arbitrary")),
        name="past_key_scale",
    )(page_table.reshape(-1), wukt, *([cache_latent] * n_pg))


def _qside_kernel(x_ref, ng_ref, wdq_ref, qlg_ref, wn_ref, wr_ref, qng_ref, kng_ref, qrg_ref, cs_ref, wukt_ref,
                  qa_ref, qr_ref):
    h = _rms(x_ref[...], ng_ref[...]).astype(BF16)
    cq = _rms(_dot(h, wdq_ref[...]), qlg_ref[...]).astype(BF16)
    qn_all = _dot(cq, wn_ref[...])
    qr_all = _dot(cq, wr_ref[...])
    cs = cs_ref[...]
    for hh in range(N_HEADS):
        cols = slice(hh * LANES, (hh + 1) * LANES)
        qn = (_rms(qn_all[:, cols], qng_ref[...]) * kng_ref[...]).astype(BF16)
        qa_ref[hh] = _dot(qn, wukt_ref[hh]).astype(BF16)
        qr_ref[hh] = _rope_pairs(qr_all[:, cols], qrg_ref[...], cs).astype(BF16)


def _q_side(x, tm, cs_map, ng, w_dq, qlg, wn, wr2, qng, kng, qrg2, cs_tab, wukt3):
    t, d = x.shape
    r = wukt3.shape[2]
    return pl.pallas_call(
        _qside_kernel,
        grid=(t // tm,),
        in_specs=[pl.BlockSpec((tm, d), lambda i: (i, 0)),
                  _resident(ng.shape), _resident(w_dq.shape), _resident(qlg.shape), _resident(wn.shape),
                  _resident(wr2.shape), _resident(qng.shape), _resident(kng.shape), _resident(qrg2.shape),
                  pl.BlockSpec((tm, LANES), lambda i: (cs_map(i), 0)),
                  _resident(wukt3.shape)],
        out_specs=[pl.BlockSpec((N_HEADS, tm, r), lambda i: (0, i, 0)),
                   pl.BlockSpec((N_HEADS, tm, LANES), lambda i: (0, i, 0))],
        out_shape=[jax.ShapeDtypeStruct((N_HEADS, t, r), BF16), jax.ShapeDtypeStruct((N_HEADS, t, LANES), BF16)],
        compiler_params=_params(("arbitrary",)),
        name="q_side",
    )(x, ng, w_dq, qlg, wn, wr2, qng, kng, qrg2, cs_tab, wukt3)


def _softmax_step(s, cb, m, l, acc):
    m_new = jnp.maximum(m, jnp.max(s, axis=1, keepdims=True))
    alpha = jnp.exp(m - m_new)
    p = jnp.exp(s - m_new)
    l_new = alpha * l + jnp.sum(p, axis=1, keepdims=True)
    acc_new = alpha * acc + _dot(p.astype(BF16), cb)
    return m_new, l_new, acc_new


def _scores(s1, s2, rt, scale):
    nrow, nk = s1.shape
    s1 = s1.reshape(N_HEADS, nrow // N_HEADS, nk) * rt[:, None, :]
    return (s1.reshape(nrow, nk) + s2) * scale


def _prompt_attn_kernel(bq, bk, scale, qa_ref, qr_ref, cb_ref, krp_ref, rt_ref, o_ref, m_ref, acc_ref):
    i = pl.program_id(1)
    r = qa_ref.shape[2]
    q = qa_ref[...].reshape(N_HEADS * bq, r)
    qr = qr_ref[...].reshape(N_HEADS * bq, LANES)
    m_ref[...] = jnp.full_like(m_ref, -jnp.inf)
    acc_ref[...] = jnp.zeros_like(acc_ref)
    qpos = i * bq + lax.broadcasted_iota(jnp.int32, (1, bq, bk), 1)
    kidx = lax.broadcasted_iota(jnp.int32, (1, bq, bk), 2)
    ones = jnp.ones((bk, LANES), BF16)

    def step(j, masked):
        k0 = pl.multiple_of(j * bk, bk)
        cb = cb_ref[pl.ds(k0, bk), :]
        rts = rt_ref[:, pl.ds(k0, bk)] * scale
        s = _dot_nt(q, cb).reshape(N_HEADS, bq, bk) * rts[:, None, :]
        s = s + (_dot_nt(qr, krp_ref[pl.ds(k0, bk), :]) * scale).reshape(N_HEADS, bq, bk)
        if masked:
            s = jnp.where(kidx + k0 <= qpos, s, -jnp.inf)
        s = s.reshape(N_HEADS * bq, bk)
        m_prev = m_ref[...]
        m_new = jnp.maximum(m_prev, jnp.max(s, axis=1, keepdims=True))
        alpha = jnp.exp(m_prev - m_new)
        p = jnp.exp(s - jnp.tile(m_new, (1, bk // LANES))).astype(BF16)
        m_ref[...] = m_new
        acc_ref[...] = (jnp.tile(alpha, (1, acc_ref.shape[1] // LANES)) * acc_ref[...]
                        + _dot(p, jnp.concatenate([cb, ones], axis=1)))

    def run(masked):
        def body(j, carry):
            step(j, masked)
            return carry
        return body

    n_visible = (i * bq) // bk
    n_blocks = (i * bq + bq + bk - 1) // bk
    lax.fori_loop(0, n_visible, run(False), 0)
    lax.fori_loop(n_visible, n_blocks, run(True), 0)
    denom = jnp.tile(acc_ref[:, r:], (1, r // LANES))
    o_ref[...] = (acc_ref[:, :r] / denom).reshape(N_HEADS, bq, r).astype(BF16)


def _prompt_attention(qa, qr, cb, krp, rt, bsz, seq, bq, bk, scale):
    r = cb.shape[1]
    nq = seq // bq
    nrow = N_HEADS * bq
    qmap = lambda b, i: (0, b * nq + i, 0)
    return pl.pallas_call(
        functools.partial(_prompt_attn_kernel, bq, bk, scale),
        grid=(bsz, nq),
        in_specs=[pl.BlockSpec((N_HEADS, bq, r), qmap), pl.BlockSpec((N_HEADS, bq, LANES), qmap),
                  pl.BlockSpec((seq, r), lambda b, i: (b, 0)), pl.BlockSpec((seq, LANES), lambda b, i: (b, 0)),
                  pl.BlockSpec((N_HEADS, seq), lambda b, i: (0, b))],
        out_specs=pl.BlockSpec((N_HEADS, bq, r), qmap),
        out_shape=jax.ShapeDtypeStruct((N_HEADS, bsz * seq, r), BF16),
        scratch_shapes=[pltpu.VMEM((nrow, LANES), F32), pltpu.VMEM((nrow, r + LANES), F32)],
        compiler_params=_params(("arbitrary", "arbitrary")),
        name="prompt_attention",
    )(qa, qr, cb, krp, rt)


def _sample_attn_kernel(n_pg, n_new, scale, pt_ref, *refs):
    qa_ref, qr_ref, rtp_ref, cn_ref, krn_ref, rtn_ref = refs[:6]
    cpages = refs[6:6 + n_pg]
    kpages = refs[6 + n_pg:6 + 2 * n_pg]
    o_ref, m_ref, l_ref, acc_ref, cblk_ref, kblk_ref = refs[6 + 2 * n_pg:]
    j = pl.program_id(1)

    @pl.when(j == 0)
    def _():
        m_ref[...] = jnp.full_like(m_ref, -jnp.inf)
        l_ref[...] = jnp.zeros_like(l_ref)
        acc_ref[...] = jnp.zeros_like(acc_ref)

    for g in range(n_pg):
        cblk_ref[g * CHUNK:(g + 1) * CHUNK, :] = cpages[g][0].astype(BF16)
        kblk_ref[:, g * CHUNK:(g + 1) * CHUNK] = kpages[g][0].astype(BF16)
    q = qa_ref[0]
    qr = qr_ref[0][:, :D_ROPE]
    cb = cblk_ref[...]
    s = _scores(_dot_nt(q, cb), _dot(qr, kblk_ref[...]), rtp_ref[0], scale)
    m, l, acc = _softmax_step(s, cb, m_ref[...], l_ref[...], acc_ref[...])
    m_ref[...] = m
    l_ref[...] = l
    acc_ref[...] = acc

    @pl.when(j == pl.num_programs(1) - 1)
    def _():
        pad = CHUNK - n_new
        cn = jnp.concatenate([cn_ref[0], jnp.zeros((pad, cn_ref.shape[2]), F32)], axis=0).astype(BF16)
        krn = jnp.concatenate([krn_ref[0], jnp.zeros((pad, D_ROPE), F32)], axis=0).astype(BF16)
        sn = _scores(_dot_nt(q, cn), _dot_nt(qr, krn), rtn_ref[0], scale)
        nrow = q.shape[0]
        qi = lax.broadcasted_iota(jnp.int32, (N_HEADS, nrow // N_HEADS, CHUNK), 1).reshape(nrow, CHUNK)
        ki = lax.broadcasted_iota(jnp.int32, (nrow, CHUNK), 1)
        sn = jnp.where(ki <= qi, sn, -jnp.inf)
        _, l2, acc2 = _softmax_step(sn, cn, m, l, acc)
        o_ref[0] = (acc2 / l2).astype(BF16)


def _sample_attention(page_table, qa_s, qr_s, rt_past, c_new, kr_new, rt_new, cache_latent, cache_krope_t,
                      n_pg, scale):
    n_seq, n_pages = page_table.shape
    page, r = cache_latent.shape[1:]
    nrow = qa_s.shape[1]
    n_new = c_new.shape[1]
    gk = n_pg * page
    seq_blk = lambda shape: pl.BlockSpec((1,) + shape, lambda b, j, pt: (b, 0, 0))
    return pl.pallas_call(
        functools.partial(_sample_attn_kernel, n_pg, n_new, scale),
        grid_spec=pltpu.PrefetchScalarGridSpec(
            num_scalar_prefetch=1,
            grid=(n_seq, n_pages // n_pg),
            in_specs=[seq_blk((nrow, r)), seq_blk((nrow, LANES)),
                      pl.BlockSpec((1, N_HEADS, gk), lambda b, j, pt: (b, 0, j)),
                      seq_blk((n_new, r)), seq_blk((n_new, D_ROPE)), seq_blk((N_HEADS, LANES))]
            + _page_specs(cache_latent, n_pages, n_pg) + _page_specs(cache_krope_t, n_pages, n_pg),
            out_specs=seq_blk((nrow, r)),
            scratch_shapes=[pltpu.VMEM((nrow, 1), F32), pltpu.VMEM((nrow, 1), F32), pltpu.VMEM((nrow, r), F32),
                            pltpu.VMEM((gk, r), BF16), pltpu.VMEM((D_ROPE, gk), BF16)]),
        out_shape=jax.ShapeDtypeStruct((n_seq, nrow, r), BF16),
        compiler_params=_params(("arbitrary", "arbitrary")),
        name="sample_attention",
    )(page_table.reshape(-1), qa_s, qr_s, rt_past, c_new, kr_new, rt_new,
      *([cache_latent] * n_pg), *([cache_krope_t] * n_pg))


def _attn_out_kernel(n_p_tiles, x_ref, op_ref, os_ref, wuv_ref, wo_ref, xo_ref):
    i = pl.program_id(0)

    def project(o_ref):
        o = jnp.concatenate([_dot(o_ref[hh], wuv_ref[hh]) for hh in range(N_HEADS)], axis=1).astype(BF16)
        xo_ref[...] = x_ref[...] + _dot(o, wo_ref[...])

    @pl.when(i < n_p_tiles)
    def _():
        project(op_ref)

    @pl.when(i >= n_p_tiles)
    def _():
        project(os_ref)


def _attn_out(x, o_p, o_s, n_p_tiles, tm, w_uv3, w_o):
    t, d = x.shape
    r = o_p.shape[2]
    return pl.pallas_call(
        functools.partial(_attn_out_kernel, n_p_tiles),
        grid=(t // tm,),
        in_specs=[pl.BlockSpec((tm, d), lambda i: (i, 0)),
                  pl.BlockSpec((N_HEADS, tm, r), lambda i: (0, jnp.minimum(i, n_p_tiles - 1), 0)),
                  pl.BlockSpec((N_HEADS, tm, r), lambda i: (0, jnp.maximum(i - n_p_tiles, 0), 0)),
                  _resident(w_uv3.shape), _resident(w_o.shape)],
        out_specs=pl.BlockSpec((tm, d), lambda i: (i, 0)),
        out_shape=jax.ShapeDtypeStruct((t, d), F32),
        compiler_params=_params(("arbitrary",)),
        name="attn_out",
    )(x, o_p, o_s, w_uv3, w_o)


def _swap_halves(a):
    half = a.shape[-1] // 2
    return jnp.concatenate([a[..., half:], a[..., :half]], axis=-1)


def _rope_table(pos):
    half = D_ROPE // 2
    inv = ROPE_THETA ** (-jnp.arange(half, dtype=F32) / half)
    ang = pos.astype(F32)[:, None] * inv[None, :]
    cos, sin = jnp.cos(ang), jnp.sin(ang)
    return jnp.concatenate([cos, cos, -sin, sin], axis=1)


def _spatial_weights(w_s, b_s, length, gw):
    reps = CHUNK // length
    w = w_s[:, :length, :length] * jnp.tril(jnp.ones((length, length), w_s.dtype))
    w_eff = jnp.einsum("st,gpq->gsptq", jnp.eye(reps, dtype=w.dtype), w).reshape(N_GROUPS, CHUNK, CHUNK)
    b_eff = jnp.repeat(jnp.tile(b_s[:, :length].T, (reps, 1)), gw, axis=1)
    return w_eff, b_eff


def kernel(x_prompt, x_sample, cache_latent, cache_krope, page_table, a_norm_g, a_w_in, a_b_in, a_sgu_g, a_w_s, a_b_s, a_w_out, a_b_out, kv_norm_g, w_dkv, kv_lat_g, w_kr, kr_g, kn_g, w_uk, w_uv, b_norm_g, b_w_dq, b_q_lat_g, b_w_uq, b_qn_g, b_qr_g, b_w_o, f_norm_g, d_w_gate, d_w_up, d_w_down, m_w_router, m_w_gate, m_w_up, m_w_down):
    bsz, seq, d = x_prompt.shape
    n_seq, dec = x_sample.shape[:2]
    t_p, t_s = bsz * seq, n_seq * dec
    n_pages = page_table.shape[1]
    past_len = n_pages * cache_latent.shape[1]
    depth = f_norm_g.shape[0]
    n_a = a_w_in.shape[0]
    dg = a_sgu_g.shape[1]
    kv_rank = w_dkv.shape[1]
    scale = float((D_NOPE + D_ROPE) ** -0.5)

    tm = 512 if t_s % 512 == 0 else CHUNK
    assert t_p % tm == 0 and t_s % tm == 0 and seq % tm == 0 and tm % CHUNK == 0
    assert seq % CHUNK == 0 and CHUNK % dec == 0 and dec <= CHUNK and tm % dec == 0
    n_p_tiles = t_p // tm
    n_pg_attn = next(g for g in (32, 16, 8, 4, 2, 1) if n_pages % g == 0)
    n_pg_scale = next(g for g in (32, 16, 8, 4, 2, 1) if n_pages % g == 0)
    kv_block = next(k for k in (512, 256, CHUNK) if seq % k == 0)
    row = lambda a: a.reshape(1, -1)
    bf = lambda a: a.astype(BF16)

    assert n_a >= 1
    x = None

    pos_p = jnp.arange(seq, dtype=jnp.int32)
    pos_s = past_len + jnp.arange(dec, dtype=jnp.int32)
    cs_tab = jnp.concatenate([_rope_table(pos_p), jnp.tile(_rope_table(pos_s), (tm // dec, 1))], axis=0)
    tiles_per_seq = seq // tm
    cs_map = lambda i: jnp.where(i < n_p_tiles, i % tiles_per_seq, tiles_per_seq)

    cache_krope_t = jnp.swapaxes(cache_krope, 1, 2)
    wukt3 = bf(jnp.transpose(w_uk, (1, 2, 0)))
    wukt = wukt3.reshape(N_HEADS * D_NOPE, kv_rank)
    w_uv3 = bf(jnp.transpose(w_uv, (1, 0, 2)))
    m_wg, m_wu, m_wd = bf(m_w_gate), bf(m_w_up), bf(m_w_down)

    v_rows = []
    c = kr = cb = krp = rt = rt_past = None
    for layer in range(depth):
        if layer < n_a:
            a = layer
            ws_p, bs_p = _spatial_weights(a_w_s[a], a_b_s[a], min(seq, CHUNK), dg // N_GROUPS)
            ws_s, bs_s = _spatial_weights(a_w_s[a], a_b_s[a], min(dec, CHUNK), dg // N_GROUPS)
            x_in = [x_prompt.reshape(t_p, d), x_sample.reshape(t_s, d)] if layer == 0 else [x]
            x, v = _gmlp_layer(x_in, n_p_tiles, tm, row(a_norm_g[a]), bf(a_w_in[a]), row(a_b_in[a]), row(a_sgu_g[a]),
                               bf(jnp.stack([ws_p, ws_s])), jnp.stack([bs_p, bs_s]), bf(a_w_out[a]), row(a_b_out[a]))
            v_rows.append(v.reshape(n_seq, dec, dg))
        else:
            if layer == n_a:
                w_kr2 = bf(jnp.concatenate([w_kr, _swap_halves(w_kr)], axis=1))
                krg2 = row(jnp.concatenate([kr_g, _swap_halves(kr_g)]))
                c, kr, cb, krp, rt = _key_side(x, tm, cs_map, row(kv_norm_g), bf(w_dkv), row(kv_lat_g), w_kr2, krg2,
                                               cs_tab, wukt)
                rt_past = _past_scale(page_table, cache_latent, wukt, n_pg_scale)
                c_new = c[t_p:].reshape(n_seq, dec, kv_rank)
                kr_new = kr[t_p:].reshape(n_seq, dec, D_ROPE)
                rt_new = jnp.transpose(rt[:, t_p:].reshape(N_HEADS, n_seq, dec), (1, 0, 2))
                rt_new = jnp.pad(rt_new, ((0, 0), (0, 0), (0, LANES - dec)))
            b = layer - n_a
            w_uq = b_w_uq[b]
            wn = bf(w_uq[:, :, :D_NOPE].reshape(w_uq.shape[0], N_HEADS * D_NOPE))
            wr = w_uq[:, :, D_NOPE:]
            wr2 = bf(jnp.concatenate([wr, _swap_halves(wr)], axis=2).reshape(w_uq.shape[0], N_HEADS * LANES))
            qrg2 = row(jnp.concatenate([b_qr_g[b], _swap_halves(b_qr_g[b])]))
            qa, qr = _q_side(x, tm, cs_map, row(b_norm_g[b]), bf(b_w_dq[b]), row(b_q_lat_g[b]), wn, wr2,
                             row(b_qn_g[b]), row(kn_g), qrg2, cs_tab, wukt3)
            o_p = _prompt_attention(qa, qr, cb, krp, rt, bsz, seq, QBLOCK, kv_block, scale)

            def per_seq(a):
                w = a.shape[2]
                return jnp.transpose(a[:, t_p:].reshape(N_HEADS, n_seq, dec, w), (1, 0, 2, 3)).reshape(
                    n_seq, N_HEADS * dec, w)

            o_s = _sample_attention(page_table, per_seq(qa), per_seq(qr), rt_past, c_new, kr_new, rt_new,
                                    cache_latent, cache_krope_t, n_pg_attn, scale)
            o_s = jnp.transpose(o_s.reshape(n_seq, N_HEADS, dec, kv_rank), (1, 0, 2, 3)).reshape(
                N_HEADS, t_s, kv_rank)
            x = _attn_out(x, o_p, o_s, n_p_tiles, tm, w_uv3, bf(b_w_o[b]))
        i = layer // 2
        if layer % 2 == 0:
            x = _ffn_dense(x, tm, row(f_norm_g[layer]), bf(d_w_gate[i]), bf(d_w_up[i]), bf(d_w_down[i]))
        else:
            x = _moe_layer(x, tm, i, row(f_norm_g[layer]), bf(m_w_router[i].T), m_wg, m_wu, m_wd,
                           t_p if layer == depth - 1 else None)

    y_p, y_s = x if isinstance(x, (list, tuple)) else (x[:t_p], x[t_p:])
    return (y_p.reshape(bsz, seq, d), y_s.reshape(n_seq, dec, d),
            c[:t_p].reshape(bsz, seq, kv_rank), kr[:t_p].reshape(bsz, seq, D_ROPE),
            c[t_p:].reshape(n_seq, dec, kv_rank), kr[t_p:].reshape(n_seq, dec, D_ROPE),
            jnp.stack(v_rows))
```

```python
import functools

import jax
import jax.numpy as jnp
from jax import lax
from jax.experimental import pallas as pl
from jax.experimental.pallas import tpu as pltpu

EPS = 1e-6
ROPE_THETA = 10000.0
CHUNK = 128
N_GROUPS = 8
N_HEADS = 8
D_NOPE = 128
D_ROPE = 64
D_V = 128
TOP_K = 2
QBLOCK = 128
LANES = 128
VMEM_LIMIT_BYTES = 60 * 1024 * 1024
GMLP_COL_CHUNK = 512

BF16 = jnp.bfloat16
F32 = jnp.float32
NT_DIMS = (((1,), (1,)), ((), ()))


def _dot(a, b):
    return jnp.dot(a, b, preferred_element_type=F32)


def _dot_nt(a, b):
    return lax.dot_general(a, b, NT_DIMS, preferred_element_type=F32)


def _rms(x, g):
    ms = jnp.mean(x * x, axis=-1, keepdims=True)
    return x * lax.rsqrt(ms + EPS) * g


def _rope_pairs(a, g2, cs):
    ms = jnp.sum(a * a, axis=-1, keepdims=True) * (1.0 / LANES)
    t = a * lax.rsqrt(ms + EPS) * g2 * cs
    return t + pltpu.roll(t, D_ROPE, 1)


def _resident(shape):
    nd = len(shape)
    return pl.BlockSpec(shape, lambda *_: (0,) * nd, pipeline_mode=pl.Buffered(1))


def _params(sem):
    return pltpu.CompilerParams(dimension_semantics=sem, vmem_limit_bytes=VMEM_LIMIT_BYTES)


def _gmlp_kernel(n_p_tiles, n_x, *refs):
    x_refs = refs[:n_x]
    ng_ref, win_ref, bin_ref, sg_ref, ws_ref, bs_ref, wout_ref, bout_ref = refs[n_x:n_x + 8]
    xo_ref, v_ref, zu_ref, vb_ref, gated_ref = refs[n_x + 8:]
    i = pl.program_id(0)
    x = x_refs[0][...] if n_x == 1 else jnp.where(i < n_p_tiles, x_refs[0][...], x_refs[1][...])
    h = _rms(x, ng_ref[...]).astype(BF16)
    dg = sg_ref.shape[1]
    ssq = jnp.zeros((x.shape[0], 1), F32)
    for c0 in range(0, dg, GMLP_COL_CHUNK):
        cols = slice(c0, c0 + GMLP_COL_CHUNK)
        vcols = slice(dg + c0, dg + c0 + GMLP_COL_CHUNK)
        zv = jax.nn.gelu(_dot(h, win_ref[:, vcols]) + bin_ref[:, vcols])
        ssq = ssq + jnp.sum(zv * zv, axis=1, keepdims=True)
        v_ref[:, cols] = zv
        zu_ref[:, cols] = jax.nn.gelu(_dot(h, win_ref[:, cols]) + bin_ref[:, cols])
    v = v_ref[...] * lax.rsqrt(ssq * (1.0 / dg) + EPS) * sg_ref[...]
    vb_ref[...] = v.astype(BF16)
    v_ref[...] = v

    gw = dg // N_GROUPS
    for c in range(x.shape[0] // CHUNK):
        rows = slice(c * CHUNK, (c + 1) * CHUNK)
        for g in range(N_GROUPS):
            cols = slice(g * gw, (g + 1) * gw)
            mix = _dot(ws_ref[0, g], vb_ref[rows, cols]) + bs_ref[0, :, cols]
            gated_ref[rows, cols] = (zu_ref[rows, cols] * mix).astype(BF16)
    xo_ref[...] = x + _dot(gated_ref[...], wout_ref[...]) + bout_ref[...]


def _gmlp_layer(xs, n_p_tiles, tm, ng, w_in, b_in, sg, ws_eff, bs_eff, w_out, b_out):
    d = xs[0].shape[1]
    t = sum(a.shape[0] for a in xs)
    dg = sg.shape[1]
    n_tiles = t // tm
    t_s = t - n_p_tiles * tm
    sel = lambda i: jnp.where(i >= n_p_tiles, 1, 0)
    if len(xs) == 1:
        x_specs = [pl.BlockSpec((tm, d), lambda i: (i, 0))]
    else:
        x_specs = [pl.BlockSpec((tm, d), lambda i: (jnp.minimum(i, n_p_tiles - 1), 0)),
                   pl.BlockSpec((tm, d), lambda i: (jnp.maximum(i - n_p_tiles, 0), 0))]
    return pl.pallas_call(
        functools.partial(_gmlp_kernel, n_p_tiles, len(xs)),
        grid=(n_tiles,),
        in_specs=x_specs + [
            _resident(ng.shape), _resident(w_in.shape), _resident(b_in.shape), _resident(sg.shape),
            pl.BlockSpec((1,) + ws_eff.shape[1:], lambda i: (sel(i), 0, 0, 0)),
            pl.BlockSpec((1,) + bs_eff.shape[1:], lambda i: (sel(i), 0, 0)),
            _resident(w_out.shape), _resident(b_out.shape),
        ],
        out_specs=[
            pl.BlockSpec((tm, d), lambda i: (i, 0)),
            pl.BlockSpec((tm, dg), lambda i: (jnp.maximum(i - n_p_tiles, 0), 0)),
        ],
        out_shape=[jax.ShapeDtypeStruct((t, d), F32), jax.ShapeDtypeStruct((t_s, dg), F32)],
        scratch_shapes=[pltpu.VMEM((tm, dg), F32), pltpu.VMEM((tm, dg), BF16), pltpu.VMEM((tm, dg), BF16)],
        compiler_params=_params(("arbitrary",)),
        name="gmlp_layer",
    )(*xs, ng, w_in, b_in, sg, ws_eff, bs_eff, w_out, b_out)


def _ffn_kernel(x_ref, ng_ref, wg_ref, wu_ref, wd_ref, xo_ref):
    x = x_ref[...]
    h = _rms(x, ng_ref[...]).astype(BF16)
    t = (jax.nn.silu(_dot(h, wg_ref[...])) * _dot(h, wu_ref[...])).astype(BF16)
    xo_ref[...] = x + _dot(t, wd_ref[...])


def _ffn_dense(x, tm, ng, wg, wu, wd):
    t, d = x.shape
    return pl.pallas_call(
        _ffn_kernel,
        grid=(t // tm,),
        in_specs=[pl.BlockSpec((tm, d), lambda i: (i, 0)),
                  _resident(ng.shape), _resident(wg.shape), _resident(wu.shape), _resident(wd.shape)],
        out_specs=pl.BlockSpec((tm, d), lambda i: (i, 0)),
        out_shape=jax.ShapeDtypeStruct((t, d), F32),
        compiler_params=_params(("arbitrary",)),
        name="ffn_dense",
    )(x, ng, wg, wu, wd)


MOE_ROW_TILE = 512
DISPATCH_TOKENS = 1024
COMBINE_TOKENS = 256


def _router_kernel(x_ref, ng_ref, wrt_ref, tri_ref, idx_ref, gate_ref, rank_ref, cnt_ref, run_ref):
    @pl.when(pl.program_id(0) == 0)
    def _():
        run_ref[...] = jnp.zeros_like(run_ref)

    hb = _rms(x_ref[...], ng_ref[...]).astype(BF16)
    logits = _dot_nt(wrt_ref[...], hb)
    ne = logits.shape[0]
    ex = jnp.exp(logits - jnp.max(logits, axis=0, keepdims=True))
    probs = ex / jnp.sum(ex, axis=0, keepdims=True)
    row = lax.broadcasted_iota(jnp.int32, probs.shape, 0)
    m1 = jnp.max(probs, axis=0, keepdims=True)
    i1 = jnp.min(jnp.where(probs == m1, row, ne), axis=0, keepdims=True)
    rest = jnp.where(row == i1, -1.0, probs)
    m2 = jnp.max(rest, axis=0, keepdims=True)
    i2 = jnp.min(jnp.where(rest == m2, row, ne), axis=0, keepdims=True)
    den = m1 + m2
    oh1 = row == i1
    oh2 = row == i2
    oh = jnp.where(oh1 | oh2, 1.0, 0.0)
    rank = _dot(oh.astype(BF16), tri_ref[...]) + run_ref[:, :1]
    idx_ref[...] = jnp.concatenate([i1, i2], axis=0)
    gate_ref[...] = jnp.concatenate([m1 / den, m2 / den], axis=0)
    rank_ref[...] = jnp.concatenate([jnp.sum(jnp.where(oh1, rank, 0.0), axis=0, keepdims=True),
                                     jnp.sum(jnp.where(oh2, rank, 0.0), axis=0, keepdims=True)],
                                    axis=0).astype(jnp.int32)
    run = run_ref[...] + jnp.sum(oh, axis=1, keepdims=True)
    run_ref[...] = run
    cnt_ref[...] = run.astype(jnp.int32)


def _router(x, tm, ng, wrt):
    t, d = x.shape
    ne = wrt.shape[0]
    tri = jnp.triu(jnp.ones((tm, tm), BF16), k=1)
    tok = lambda i: (0, i)
    return pl.pallas_call(
        _router_kernel,
        grid=(t // tm,),
        in_specs=[pl.BlockSpec((tm, d), lambda i: (i, 0)), _resident(ng.shape), _resident(wrt.shape),
                  _resident(tri.shape)],
        out_specs=[pl.BlockSpec((TOP_K, tm), tok), pl.BlockSpec((TOP_K, tm), tok), pl.BlockSpec((TOP_K, tm), tok),
                   pl.BlockSpec((ne, LANES), lambda i: (0, 0))],
        out_shape=[jax.ShapeDtypeStruct((TOP_K, t), jnp.int32), jax.ShapeDtypeStruct((TOP_K, t), F32),
                   jax.ShapeDtypeStruct((TOP_K, t), jnp.int32), jax.ShapeDtypeStruct((ne, LANES), jnp.int32)],
        scratch_shapes=[pltpu.VMEM((ne, LANES), F32)],
        compiler_params=_params(("arbitrary",)),
        name="moe_router",
    )(x, ng, wrt, tri)


def _row_copy(src_ref, src_row, dst_ref, dst_row, sem):
    return pltpu.make_async_copy(src_ref.at[pl.ds(src_row, 1)], dst_ref.at[pl.ds(dst_row, 1)], sem)


def _dispatch_kernel(fill_lo_ref, fill_hi_ref, pos_ref, x_ref, xs_hbm, zero_ref, sem, zsem):
    n = pos_ref.shape[1]

    @pl.when(pl.program_id(0) == 0)
    def _():
        zero_ref[...] = jnp.zeros_like(zero_ref)

        def fill(row, carry):
            _row_copy(zero_ref, 0, xs_hbm, row, zsem).start()
            return carry

        def drain(row, carry):
            _row_copy(zero_ref, 0, xs_hbm, row, zsem).wait()
            return carry

        for e in range(fill_lo_ref.shape[0]):
            lax.fori_loop(fill_lo_ref[e], fill_hi_ref[e], fill, 0)
        for e in range(fill_lo_ref.shape[0]):
            lax.fori_loop(fill_lo_ref[e], fill_hi_ref[e], drain, 0)

    def issue(r, carry):
        for k in range(TOP_K):
            _row_copy(x_ref, r, xs_hbm, pos_ref[k, r], sem).start()
        return carry

    lax.fori_loop(0, n, issue, 0, unroll=8)
    for k in range(TOP_K):
        pltpu.make_async_copy(x_ref, xs_hbm.at[pl.ds(0, n)], sem).wait()


def _dispatch(x, pos, fill_lo, fill_hi, n_rows):
    t, d = x.shape
    n = DISPATCH_TOKENS if t % DISPATCH_TOKENS == 0 else LANES
    return pl.pallas_call(
        _dispatch_kernel,
        grid_spec=pltpu.PrefetchScalarGridSpec(
            num_scalar_prefetch=2,
            grid=(t // n,),
            in_specs=[pl.BlockSpec((TOP_K, n), lambda i, lo, hi: (0, i), memory_space=pltpu.SMEM),
                      pl.BlockSpec((n, d), lambda i, lo, hi: (i, 0))],
            out_specs=pl.BlockSpec(memory_space=pl.ANY),
            scratch_shapes=[pltpu.VMEM((8, d), F32), pltpu.SemaphoreType.DMA(()), pltpu.SemaphoreType.DMA(())]),
        out_shape=jax.ShapeDtypeStruct((n_rows, d), F32),
        compiler_params=_params(("arbitrary",)),
        name="moe_dispatch",
    )(fill_lo, fill_hi, pos, x)


def _grouped_ffn_kernel(layer, te_ref, nu_ref, xs_ref, ng_ref, wg_ref, wu_ref, wd_ref, ys_ref):
    del layer, te_ref
    i = pl.program_id(0)

    @pl.when(i < nu_ref[0])
    def _():
        h = _rms(xs_ref[...], ng_ref[...]).astype(BF16)
        t = (jax.nn.silu(_dot(h, wg_ref[0, 0])) * _dot(h, wu_ref[0, 0])).astype(BF16)
        ys_ref[...] = _dot(t, wd_ref[0, 0])

    @pl.when(i >= nu_ref[0])
    def _():
        ys_ref[...] = jnp.zeros_like(ys_ref)


def _grouped_ffn(xs, tile_expert, n_used, layer, ng, wg, wu, wd):
    n_rows, d = xs.shape
    dff = wg.shape[3]
    tmg = MOE_ROW_TILE
    wspec = lambda shape: pl.BlockSpec((1, 1) + shape, lambda i, te, nu: (layer, te[i], 0, 0),
                                       pipeline_mode=pl.Buffered(1))
    return pl.pallas_call(
        functools.partial(_grouped_ffn_kernel, layer),
        grid_spec=pltpu.PrefetchScalarGridSpec(
            num_scalar_prefetch=2,
            grid=(n_rows // tmg,),
            in_specs=[pl.BlockSpec((tmg, d), lambda i, te, nu: (jnp.minimum(i, jnp.maximum(nu[0], 1) - 1), 0)),
                      pl.BlockSpec(ng.shape, lambda i, te, nu: (0, 0), pipeline_mode=pl.Buffered(1)),
                      wspec((d, dff)), wspec((d, dff)), wspec((dff, d))],
            out_specs=pl.BlockSpec((tmg, d), lambda i, te, nu: (i, 0))),
        out_shape=jax.ShapeDtypeStruct((n_rows, d), F32),
        compiler_params=_params(("arbitrary",)),
        name="moe_grouped_ffn",
    )(tile_expert, n_used, xs, ng, wg, wu, wd)


def _combine_kernel(n_first, pos_ref, pos_next_ref, x_ref, gate_ref, ys_hbm, *refs):
    outs, (buf_ref, sem) = refs[:-2], refs[-2:]
    n = x_ref.shape[0]
    i = pl.program_id(0)
    slot = i % 2

    def gather(p_ref, s):
        def issue(r, carry):
            for k in range(TOP_K):
                _row_copy(ys_hbm, p_ref[k, r], buf_ref.at[s, k], r, sem.at[s]).start()
            return carry
        lax.fori_loop(0, n, issue, 0, unroll=8)

    @pl.when(i == 0)
    def _():
        gather(pos_ref, 0)

    @pl.when(i + 1 < pl.num_programs(0))
    def _():
        gather(pos_next_ref, 1 - slot)

    for k in range(TOP_K):
        pltpu.make_async_copy(ys_hbm.at[pl.ds(0, n)], buf_ref.at[slot, k], sem.at[slot]).wait()
    g = gate_ref[...]
    y = x_ref[...] + (g[:, 0:1] * buf_ref[slot, 0] + g[:, 1:2] * buf_ref[slot, 1])
    if len(outs) == 1:
        outs[0][...] = y
    else:
        @pl.when(i < n_first)
        def _():
            outs[0][...] = y

        @pl.when(i >= n_first)
        def _():
            outs[1][...] = y


def _combine(x, pos, gates, ys, t_first=None):
    t, d = x.shape
    n = COMBINE_TOKENS if t % COMBINE_TOKENS == 0 else LANES
    n_steps = t // n
    if t_first is None:
        n_first = n_steps
        out_specs = pl.BlockSpec((n, d), lambda i: (i, 0))
        out_shape = jax.ShapeDtypeStruct((t, d), F32)
    else:
        assert t_first % n == 0 and 0 < t_first < t
        n_first = t_first // n
        out_specs = [pl.BlockSpec((n, d), lambda i: (jnp.minimum(i, n_first - 1), 0)),
                     pl.BlockSpec((n, d), lambda i: (jnp.maximum(i - n_first, 0), 0))]
        out_shape = [jax.ShapeDtypeStruct((t_first, d), F32), jax.ShapeDtypeStruct((t - t_first, d), F32)]
    return pl.pallas_call(
        functools.partial(_combine_kernel, n_first),
        grid=(n_steps,),
        in_specs=[pl.BlockSpec((TOP_K, n), lambda i: (0, i), memory_space=pltpu.SMEM),
                  pl.BlockSpec((TOP_K, n), lambda i: (0, jnp.minimum(i + 1, n_steps - 1)), memory_space=pltpu.SMEM),
                  pl.BlockSpec((n, d), lambda i: (i, 0)), pl.BlockSpec((n, TOP_K), lambda i: (i, 0)),
                  pl.BlockSpec(memory_space=pl.ANY)],
        out_specs=out_specs,
        out_shape=out_shape,
        scratch_shapes=[pltpu.VMEM((2, TOP_K, n, d), F32), pltpu.SemaphoreType.DMA((2,))],
        compiler_params=_params(("arbitrary",)),
        name="moe_combine",
    )(pos, pos, x, gates, ys)


def _moe_layer(x, tm, layer, ng, wrt, wg, wu, wd, t_first=None):
    t, d = x.shape
    ne = wrt.shape[0]
    tmg = MOE_ROW_TILE
    n_rows = -(-(t * TOP_K) // tmg) * tmg + ne * tmg
    idx, gate, rank, cnt = _router(x, tm, ng, wrt)
    padded = (cnt[:, 0] + tmg - 1) // tmg * tmg
    ends = jnp.cumsum(padded)
    starts = ends - padded
    pos = rank + sum(jnp.where(idx == e, starts[e], 0) for e in range(ne))
    tile_start = jnp.arange(n_rows // tmg, dtype=jnp.int32) * tmg
    tile_expert = jnp.minimum(jnp.sum(tile_start[:, None] >= ends[None, :], axis=1), ne - 1).astype(jnp.int32)
    n_used = (ends[-1:] // tmg).astype(jnp.int32)
    fill_hi = jnp.where(jnp.arange(ne) == ne - 1, n_rows, ends).astype(jnp.int32)
    xs = _dispatch(x, pos, starts + cnt[:, 0], fill_hi, n_rows)
    ys = _grouped_ffn(xs, tile_expert, n_used, layer, ng, wg, wu, wd)
    return _combine(x, pos, gate.T, ys, t_first)


def _keyside_kernel(x_ref, ng_ref, wdkv_ref, lg_ref, wkr_ref, krg_ref, cs_ref, wukt_ref,
                    c_ref, kr_ref, cb_ref, krp_ref, rt_ref):
    h = _rms(x_ref[...], ng_ref[...]).astype(BF16)
    c = _rms(_dot(h, wdkv_ref[...]), lg_ref[...])
    cb = c.astype(BF16)
    c_ref[...] = c
    cb_ref[...] = cb
    full = _rope_pairs(_dot(h, wkr_ref[...]), krg_ref[...], cs_ref[...])
    kr_ref[...] = full[:, :D_ROPE]
    lane = lax.broadcasted_iota(jnp.int32, full.shape, 1)
    krp_ref[...] = jnp.where(lane < D_ROPE, full, 0.0).astype(BF16)
    rt_ref[...] = _key_rms_scale_t(wukt_ref[...], cb)


def _key_rms_scale_t(wukt, cb):
    kt = _dot_nt(wukt, cb)
    ssq = jnp.sum((kt * kt).reshape(N_HEADS, D_NOPE, kt.shape[1]), axis=1)
    return lax.rsqrt(ssq * (1.0 / D_NOPE) + EPS)


def _key_side(x, tm, cs_map, ng, w_dkv, lg, w_kr2, krg2, cs_tab, wukt):
    t, d = x.shape
    r = w_dkv.shape[1]
    row = lambda i: (i, 0)
    return pl.pallas_call(
        _keyside_kernel,
        grid=(t // tm,),
        in_specs=[pl.BlockSpec((tm, d), row),
                  _resident(ng.shape), _resident(w_dkv.shape), _resident(lg.shape),
                  _resident(w_kr2.shape), _resident(krg2.shape),
                  pl.BlockSpec((tm, LANES), lambda i: (cs_map(i), 0)),
                  _resident(wukt.shape)],
        out_specs=[pl.BlockSpec((tm, r), row), pl.BlockSpec((tm, D_ROPE), row), pl.BlockSpec((tm, r), row),
                   pl.BlockSpec((tm, LANES), row), pl.BlockSpec((N_HEADS, tm), lambda i: (0, i))],
        out_shape=[jax.ShapeDtypeStruct((t, r), F32), jax.ShapeDtypeStruct((t, D_ROPE), F32),
                   jax.ShapeDtypeStruct((t, r), BF16), jax.ShapeDtypeStruct((t, LANES), BF16),
                   jax.ShapeDtypeStruct((N_HEADS, t), F32)],
        compiler_params=_params(("arbitrary",)),
        name="key_side",
    )(x, ng, w_dkv, lg, w_kr2, krg2, cs_tab, wukt)


def _past_scale_kernel(n_pg, pt_ref, *refs):
    wukt_ref = refs[0]
    pages = refs[1:1 + n_pg]
    rt_ref = refs[1 + n_pg]
    cblk_ref = refs[2 + n_pg]
    for g in range(n_pg):
        cblk_ref[g * CHUNK:(g + 1) * CHUNK, :] = pages[g][0].astype(BF16)
    rt_ref[0] = _key_rms_scale_t(wukt_ref[...], cblk_ref[...])


def _page_specs(cache, n_pages, n_pg):
    page, width = cache.shape[1:]
    return [pl.BlockSpec((1, page, width),
                         functools.partial(lambda g, b, j, pt: (pt[b * n_pages + j * n_pg + g], 0, 0), g))
            for g in range(n_pg)]


def _past_scale(page_table, cache_latent, wukt, n_pg):
    n_seq, n_pages = page_table.shape
    page, r = cache_latent.shape[1:]
    gk = n_pg * page
    return pl.pallas_call(
        functools.partial(_past_scale_kernel, n_pg),
        grid_spec=pltpu.PrefetchScalarGridSpec(
            num_scalar_prefetch=1,
            grid=(n_seq, n_pages // n_pg),
            in_specs=[pl.BlockSpec(wukt.shape, lambda b, j, pt: (0, 0), pipeline_mode=pl.Buffered(1))]
            + _page_specs(cache_latent, n_pages, n_pg),
            out_specs=pl.BlockSpec((1, N_HEADS, gk), lambda b, j, pt: (b, 0, j)),
            scratch_shapes=[pltpu.VMEM((gk, r), BF16)]),
        out_shape=jax.ShapeDtypeStruct((n_seq, N_HEADS, n_pages * page), F32),
        compiler_params=_params(("arbitrary", "arbitrary")),
        name="past_key_scale",
    )(page_table.reshape(-1), wukt, *([cache_latent] * n_pg))


def _qside_kernel(x_ref, ng_ref, wdq_ref, qlg_ref, wn_ref, wr_ref, qng_ref, kng_ref, qrg_ref, cs_ref, wukt_ref,
                  qa_ref, qr_ref):
    h = _rms(x_ref[...], ng_ref[...]).astype(BF16)
    cq = _rms(_dot(h, wdq_ref[...]), qlg_ref[...]).astype(BF16)
    qn_all = _dot(cq, wn_ref[...])
    qr_all = _dot(cq, wr_ref[...])
    cs = cs_ref[...]
    for hh in range(N_HEADS):
        cols = slice(hh * LANES, (hh + 1) * LANES)
        qn = (_rms(qn_all[:, cols], qng_ref[...]) * kng_ref[...]).astype(BF16)
        qa_ref[hh] = _dot(qn, wukt_ref[hh]).astype(BF16)
        qr_ref[hh] = _rope_pairs(qr_all[:, cols], qrg_ref[...], cs).astype(BF16)


def _q_side(x, tm, cs_map, ng, w_dq, qlg, wn, wr2, qng, kng, qrg2, cs_tab, wukt3):
    t, d = x.shape
    r = wukt3.shape[2]
    return pl.pallas_call(
        _qside_kernel,
        grid=(t // tm,),
        in_specs=[pl.BlockSpec((tm, d), lambda i: (i, 0)),
                  _resident(ng.shape), _resident(w_dq.shape), _resident(qlg.shape), _resident(wn.shape),
                  _resident(wr2.shape), _resident(qng.shape), _resident(kng.shape), _resident(qrg2.shape),
                  pl.BlockSpec((tm, LANES), lambda i: (cs_map(i), 0)),
                  _resident(wukt3.shape)],
        out_specs=[pl.BlockSpec((N_HEADS, tm, r), lambda i: (0, i, 0)),
                   pl.BlockSpec((N_HEADS, tm, LANES), lambda i: (0, i, 0))],
        out_shape=[jax.ShapeDtypeStruct((N_HEADS, t, r), BF16), jax.ShapeDtypeStruct((N_HEADS, t, LANES), BF16)],
        compiler_params=_params(("arbitrary",)),
        name="q_side",
    )(x, ng, w_dq, qlg, wn, wr2, qng, kng, qrg2, cs_tab, wukt3)


def _softmax_step(s, cb, m, l, acc):
    m_new = jnp.maximum(m, jnp.max(s, axis=1, keepdims=True))
    alpha = jnp.exp(m - m_new)
    p = jnp.exp(s - m_new)
    l_new = alpha * l + jnp.sum(p, axis=1, keepdims=True)
    acc_new = alpha * acc + _dot(p.astype(BF16), cb)
    return m_new, l_new, acc_new


def _scores(s1, s2, rt, scale):
    nrow, nk = s1.shape
    s1 = s1.reshape(N_HEADS, nrow // N_HEADS, nk) * rt[:, None, :]
    return (s1.reshape(nrow, nk) + s2) * scale


def _prompt_attn_kernel(bq, bk, scale, qa_ref, qr_ref, cb_ref, krp_ref, rt_ref, o_ref, m_ref, acc_ref):
    i = pl.program_id(1)
    r = qa_ref.shape[2]
    q = qa_ref[...].reshape(N_HEADS * bq, r)
    qr = qr_ref[...].reshape(N_HEADS * bq, LANES)
    m_ref[...] = jnp.full_like(m_ref, -jnp.inf)
    acc_ref[...] = jnp.zeros_like(acc_ref)
    def step(k0, w, masked):
        cb = cb_ref[pl.ds(k0, w), :]
        rts = rt_ref[:, pl.ds(k0, w)] * scale
        s = _dot_nt(q, cb).reshape(N_HEADS, bq, w) * rts[:, None, :]
        s = s + (_dot_nt(qr, krp_ref[pl.ds(k0, w), :]) * scale).reshape(N_HEADS, bq, w)
        if masked:
            qpos = i * bq + lax.broadcasted_iota(jnp.int32, (1, bq, w), 1)
            kpos = k0 + lax.broadcasted_iota(jnp.int32, (1, bq, w), 2)
            s = jnp.where(kpos <= qpos, s, -jnp.inf)
        s = s.reshape(N_HEADS * bq, w)
        m_prev = m_ref[...]
        m_new = jnp.maximum(m_prev, jnp.max(s, axis=1, keepdims=True))
        alpha = jnp.exp(m_prev - m_new)
        p = jnp.exp(s - jnp.tile(m_new, (1, w // LANES))).astype(BF16)
        m_ref[...] = m_new
        acc_ref[...] = (jnp.tile(alpha, (1, acc_ref.shape[1] // LANES)) * acc_ref[...]
                        + _dot(p, jnp.concatenate([cb, jnp.ones((w, LANES), BF16)], axis=1)))

    def full_block(j, carry):
        step(pl.multiple_of(j * bk, bk), bk, False)
        return carry

    n_visible = (i * bq) // bk
    lax.fori_loop(0, n_visible, full_block, 0)
    k_diag = pl.multiple_of(n_visible * bk, bk)
    for part in range(bk // bq):
        @pl.when(i % (bk // bq) == part)
        def _():
            step(k_diag, (part + 1) * bq, True)
    denom = jnp.tile(acc_ref[:, r:], (1, r // LANES))
    o_ref[...] = (acc_ref[:, :r] / denom).reshape(N_HEADS, bq, r).astype(BF16)


def _prompt_attention(qa, qr, cb, krp, rt, bsz, seq, bq, bk, scale):
    r = cb.shape[1]
    nq = seq // bq
    nrow = N_HEADS * bq
    qmap = lambda b, i: (0, b * nq + i, 0)
    return pl.pallas_call(
        functools.partial(_prompt_attn_kernel, bq, bk, scale),
        grid=(bsz, nq),
        in_specs=[pl.BlockSpec((N_HEADS, bq, r), qmap), pl.BlockSpec((N_HEADS, bq, LANES), qmap),
                  pl.BlockSpec((seq, r), lambda b, i: (b, 0)), pl.BlockSpec((seq, LANES), lambda b, i: (b, 0)),
                  pl.BlockSpec((N_HEADS, seq), lambda b, i: (0, b))],
        out_specs=pl.BlockSpec((N_HEADS, bq, r), qmap),
        out_shape=jax.ShapeDtypeStruct((N_HEADS, bsz * seq, r), BF16),
        scratch_shapes=[pltpu.VMEM((nrow, LANES), F32), pltpu.VMEM((nrow, r + LANES), F32)],
        compiler_params=_params(("arbitrary", "arbitrary")),
        name="prompt_attention",
    )(qa, qr, cb, krp, rt)


def _sample_attn_kernel(n_pg, n_new, scale, pt_ref, *refs):
    qa_ref, qr_ref, rtp_ref, cn_ref, krn_ref, rtn_ref = refs[:6]
    cpages = refs[6:6 + n_pg]
    kpages = refs[6 + n_pg:6 + 2 * n_pg]
    o_ref, m_ref, l_ref, acc_ref, cblk_ref, kblk_ref = refs[6 + 2 * n_pg:]
    j = pl.program_id(1)

    @pl.when(j == 0)
    def _():
        m_ref[...] = jnp.full_like(m_ref, -jnp.inf)
        l_ref[...] = jnp.zeros_like(l_ref)
        acc_ref[...] = jnp.zeros_like(acc_ref)

    for g in range(n_pg):
        cblk_ref[g * CHUNK:(g + 1) * CHUNK, :] = cpages[g][0].astype(BF16)
        kblk_ref[:, g * CHUNK:(g + 1) * CHUNK] = kpages[g][0].astype(BF16)
    q = qa_ref[0]
    qr = qr_ref[0][:, :D_ROPE]
    cb = cblk_ref[...]
    s = _scores(_dot_nt(q, cb), _dot(qr, kblk_ref[...]), rtp_ref[0], scale)
    m, l, acc = _softmax_step(s, cb, m_ref[...], l_ref[...], acc_ref[...])
    m_ref[...] = m
    l_ref[...] = l
    acc_ref[...] = acc

    @pl.when(j == pl.num_programs(1) - 1)
    def _():
        pad = CHUNK - n_new
        cn = jnp.concatenate([cn_ref[0], jnp.zeros((pad, cn_ref.shape[2]), F32)], axis=0).astype(BF16)
        krn = jnp.concatenate([krn_ref[0], jnp.zeros((pad, D_ROPE), F32)], axis=0).astype(BF16)
        sn = _scores(_dot_nt(q, cn), _dot_nt(qr, krn), rtn_ref[0], scale)
        nrow = q.shape[0]
        qi = lax.broadcasted_iota(jnp.int32, (N_HEADS, nrow // N_HEADS, CHUNK), 1).reshape(nrow, CHUNK)
        ki = lax.broadcasted_iota(jnp.int32, (nrow, CHUNK), 1)
        sn = jnp.where(ki <= qi, sn, -jnp.inf)
        _, l2, acc2 = _softmax_step(sn, cn, m, l, acc)
        o_ref[0] = (acc2 / l2).astype(BF16)


def _sample_attention(page_table, qa_s, qr_s, rt_past, c_new, kr_new, rt_new, cache_latent, cache_krope_t,
                      n_pg, scale):
    n_seq, n_pages = page_table.shape
    page, r = cache_latent.shape[1:]
    nrow = qa_s.shape[1]
    n_new = c_new.shape[1]
    gk = n_pg * page
    seq_blk = lambda shape: pl.BlockSpec((1,) + shape, lambda b, j, pt: (b, 0, 0))
    return pl.pallas_call(
        functools.partial(_sample_attn_kernel, n_pg, n_new, scale),
        grid_spec=pltpu.PrefetchScalarGridSpec(
            num_scalar_prefetch=1,
            grid=(n_seq, n_pages // n_pg),
            in_specs=[seq_blk((nrow, r)), seq_blk((nrow, LANES)),
                      pl.BlockSpec((1, N_HEADS, gk), lambda b, j, pt: (b, 0, j)),
                      seq_blk((n_new, r)), seq_blk((n_new, D_ROPE)), seq_blk((N_HEADS, LANES))]
            + _page_specs(cache_latent, n_pages, n_pg) + _page_specs(cache_krope_t, n_pages, n_pg),
            out_specs=seq_blk((nrow, r)),
            scratch_shapes=[pltpu.VMEM((nrow, 1), F32), pltpu.VMEM((nrow, 1), F32), pltpu.VMEM((nrow, r), F32),
                            pltpu.VMEM((gk, r), BF16), pltpu.VMEM((D_ROPE, gk), BF16)]),
        out_shape=jax.ShapeDtypeStruct((n_seq, nrow, r), BF16),
        compiler_params=_params(("arbitrary", "arbitrary")),
        name="sample_attention",
    )(page_table.reshape(-1), qa_s, qr_s, rt_past, c_new, kr_new, rt_new,
      *([cache_latent] * n_pg), *([cache_krope_t] * n_pg))


def _attn_out_kernel(n_p_tiles, x_ref, op_ref, os_ref, wuv_ref, wo_ref, xo_ref):
    i = pl.program_id(0)

    def project(o_ref):
        o = jnp.concatenate([_dot(o_ref[hh], wuv_ref[hh]) for hh in range(N_HEADS)], axis=1).astype(BF16)
        xo_ref[...] = x_ref[...] + _dot(o, wo_ref[...])

    @pl.when(i < n_p_tiles)
    def _():
        project(op_ref)

    @pl.when(i >= n_p_tiles)
    def _():
        project(os_ref)


def _attn_out(x, o_p, o_s, n_p_tiles, tm, w_uv3, w_o):
    t, d = x.shape
    r = o_p.shape[2]
    return pl.pallas_call(
        functools.partial(_attn_out_kernel, n_p_tiles),
        grid=(t // tm,),
        in_specs=[pl.BlockSpec((tm, d), lambda i: (i, 0)),
                  pl.BlockSpec((N_HEADS, tm, r), lambda i: (0, jnp.minimum(i, n_p_tiles - 1), 0)),
                  pl.BlockSpec((N_HEADS, tm, r), lambda i: (0, jnp.maximum(i - n_p_tiles, 0), 0)),
                  _resident(w_uv3.shape), _resident(w_o.shape)],
        out_specs=pl.BlockSpec((tm, d), lambda i: (i, 0)),
        out_shape=jax.ShapeDtypeStruct((t, d), F32),
        compiler_params=_params(("arbitrary",)),
        name="attn_out",
    )(x, o_p, o_s, w_uv3, w_o)


def _swap_halves(a):
    half = a.shape[-1] // 2
    return jnp.concatenate([a[..., half:], a[..., :half]], axis=-1)


def _rope_table(pos):
    half = D_ROPE // 2
    inv = ROPE_THETA ** (-jnp.arange(half, dtype=F32) / half)
    ang = pos.astype(F32)[:, None] * inv[None, :]
    cos, sin = jnp.cos(ang), jnp.sin(ang)
    return jnp.concatenate([cos, cos, -sin, sin], axis=1)


def _spatial_weights(w_s, b_s, length, gw):
    reps = CHUNK // length
    w = w_s[:, :length, :length] * jnp.tril(jnp.ones((length, length), w_s.dtype))
    w_eff = jnp.einsum("st,gpq->gsptq", jnp.eye(reps, dtype=w.dtype), w).reshape(N_GROUPS, CHUNK, CHUNK)
    b_eff = jnp.repeat(jnp.tile(b_s[:, :length].T, (reps, 1)), gw, axis=1)
    return w_eff, b_eff


def kernel(x_prompt, x_sample, cache_latent, cache_krope, page_table, a_norm_g, a_w_in, a_b_in, a_sgu_g, a_w_s, a_b_s, a_w_out, a_b_out, kv_norm_g, w_dkv, kv_lat_g, w_kr, kr_g, kn_g, w_uk, w_uv, b_norm_g, b_w_dq, b_q_lat_g, b_w_uq, b_qn_g, b_qr_g, b_w_o, f_norm_g, d_w_gate, d_w_up, d_w_down, m_w_router, m_w_gate, m_w_up, m_w_down):
    bsz, seq, d = x_prompt.shape
    n_seq, dec = x_sample.shape[:2]
    t_p, t_s = bsz * seq, n_seq * dec
    n_pages = page_table.shape[1]
    past_len = n_pages * cache_latent.shape[1]
    depth = f_norm_g.shape[0]
    n_a = a_w_in.shape[0]
    dg = a_sgu_g.shape[1]
    kv_rank = w_dkv.shape[1]
    scale = float((D_NOPE + D_ROPE) ** -0.5)

    tm = 512 if t_s % 512 == 0 else CHUNK
    assert t_p % tm == 0 and t_s % tm == 0 and seq % tm == 0 and tm % CHUNK == 0
    assert seq % CHUNK == 0 and CHUNK % dec == 0 and dec <= CHUNK and tm % dec == 0
    n_p_tiles = t_p // tm
    n_pg_attn = next(g for g in (32, 16, 8, 4, 2, 1) if n_pages % g == 0)
    n_pg_scale = next(g for g in (32, 16, 8, 4, 2, 1) if n_pages % g == 0)
    kv_block = next(k for k in (512, 256, CHUNK) if seq % k == 0)
    row = lambda a: a.reshape(1, -1)
    bf = lambda a: a.astype(BF16)

    assert n_a >= 1
    x = None

    pos_p = jnp.arange(seq, dtype=jnp.int32)
    pos_s = past_len + jnp.arange(dec, dtype=jnp.int32)
    cs_tab = jnp.concatenate([_rope_table(pos_p), jnp.tile(_rope_table(pos_s), (tm // dec, 1))], axis=0)
    tiles_per_seq = seq // tm
    cs_map = lambda i: jnp.where(i < n_p_tiles, i % tiles_per_seq, tiles_per_seq)

    cache_krope_t = jnp.swapaxes(cache_krope, 1, 2)
    wukt3 = bf(jnp.transpose(w_uk, (1, 2, 0)))
    wukt = wukt3.reshape(N_HEADS * D_NOPE, kv_rank)
    w_uv3 = bf(jnp.transpose(w_uv, (1, 0, 2)))
    m_wg, m_wu, m_wd = bf(m_w_gate), bf(m_w_up), bf(m_w_down)

    v_rows = []
    c = kr = cb = krp = rt = rt_past = None
    for layer in range(depth):
        if layer < n_a:
            a = layer
            ws_p, bs_p = _spatial_weights(a_w_s[a], a_b_s[a], min(seq, CHUNK), dg // N_GROUPS)
            ws_s, bs_s = _spatial_weights(a_w_s[a], a_b_s[a], min(dec, CHUNK), dg // N_GROUPS)
            x_in = [x_prompt.reshape(t_p, d), x_sample.reshape(t_s, d)] if layer == 0 else [x]
            x, v = _gmlp_layer(x_in, n_p_tiles, tm, row(a_norm_g[a]), bf(a_w_in[a]), row(a_b_in[a]), row(a_sgu_g[a]),
                               bf(jnp.stack([ws_p, ws_s])), jnp.stack([bs_p, bs_s]), bf(a_w_out[a]), row(a_b_out[a]))
            v_rows.append(v.reshape(n_seq, dec, dg))
        else:
            if layer == n_a:
                w_kr2 = bf(jnp.concatenate([w_kr, _swap_halves(w_kr)], axis=1))
                krg2 = row(jnp.concatenate([kr_g, _swap_halves(kr_g)]))
                c, kr, cb, krp, rt = _key_side(x, tm, cs_map, row(kv_norm_g), bf(w_dkv), row(kv_lat_g), w_kr2, krg2,
                                               cs_tab, wukt)
                rt_past = _past_scale(page_table, cache_latent, wukt, n_pg_scale)
                c_new = c[t_p:].reshape(n_seq, dec, kv_rank)
                kr_new = kr[t_p:].reshape(n_seq, dec, D_ROPE)
                rt_new = jnp.transpose(rt[:, t_p:].reshape(N_HEADS, n_seq, dec), (1, 0, 2))
                rt_new = jnp.pad(rt_new, ((0, 0), (0, 0), (0, LANES - dec)))
            b = layer - n_a
            w_uq = b_w_uq[b]
            wn = bf(w_uq[:, :, :D_NOPE].reshape(w_uq.shape[0], N_HEADS * D_NOPE))
            wr = w_uq[:, :, D_NOPE:]
            wr2 = bf(jnp.concatenate([wr, _swap_halves(wr)], axis=2).reshape(w_uq.shape[0], N_HEADS * LANES))
            qrg2 = row(jnp.concatenate([b_qr_g[b], _swap_halves(b_qr_g[b])]))
            qa, qr = _q_side(x, tm, cs_map, row(b_norm_g[b]), bf(b_w_dq[b]), row(b_q_lat_g[b]), wn, wr2,
                             row(b_qn_g[b]), row(kn_g), qrg2, cs_tab, wukt3)
            o_p = _prompt_attention(qa, qr, cb, krp, rt, bsz, seq, QBLOCK, kv_block, scale)

            def per_seq(a):
                w = a.shape[2]
                return jnp.transpose(a[:, t_p:].reshape(N_HEADS, n_seq, dec, w), (1, 0, 2, 3)).reshape(
                    n_seq, N_HEADS * dec, w)

            o_s = _sample_attention(page_table, per_seq(qa), per_seq(qr), rt_past, c_new, kr_new, rt_new,
                                    cache_latent, cache_krope_t, n_pg_attn, scale)
            o_s = jnp.transpose(o_s.reshape(n_seq, N_HEADS, dec, kv_rank), (1, 0, 2, 3)).reshape(
                N_HEADS, t_s, kv_rank)
            x = _attn_out(x, o_p, o_s, n_p_tiles, tm, w_uv3, bf(b_w_o[b]))
        i = layer // 2
        if layer % 2 == 0:
            x = _ffn_dense(x, tm, row(f_norm_g[layer]), bf(d_w_gate[i]), bf(d_w_up[i]), bf(d_w_down[i]))
        else:
            x = _moe_layer(x, tm, i, row(f_norm_g[layer]), bf(m_w_router[i].T), m_wg, m_wu, m_wd,
                           t_p if layer == depth - 1 else None)

    y_p, y_s = x if isinstance(x, (list, tuple)) else (x[:t_p], x[t_p:])
    return (y_p.reshape(bsz, seq, d), y_s.reshape(n_seq, dec, d),
            c[:t_p].reshape(bsz, seq, kv_rank), kr[:t_p].reshape(bsz, seq, D_ROPE),
            c[t_p:].reshape(n_seq, dec, kv_rank), kr[t_p:].reshape(n_seq, dec, D_ROPE),
            jnp.stack(v_rows))
```

```python
import functools

import jax
import jax.numpy as jnp
from jax import lax
from jax.experimental import pallas as pl
from jax.experimental.pallas import tpu as pltpu

EPS = 1e-6
ROPE_THETA = 10000.0
CHUNK = 128
N_GROUPS = 8
N_HEADS = 8
D_NOPE = 128
D_ROPE = 64
D_V = 128
TOP_K = 2
QBLOCK = 128
LANES = 128
VMEM_LIMIT_BYTES = 60 * 1024 * 1024
GMLP_COL_CHUNK = 512

BF16 = jnp.bfloat16
F32 = jnp.float32
NT_DIMS = (((1,), (1,)), ((), ()))


def _dot(a, b):
    return jnp.dot(a, b, preferred_element_type=F32)


def _dot_nt(a, b):
    return lax.dot_general(a, b, NT_DIMS, preferred_element_type=F32)


def _rms(x, g):
    ms = jnp.mean(x * x, axis=-1, keepdims=True)
    return x * lax.rsqrt(ms + EPS) * g


def _rope_pairs(a, g2, cs):
    ms = jnp.sum(a * a, axis=-1, keepdims=True) * (1.0 / LANES)
    t = a * lax.rsqrt(ms + EPS) * g2 * cs
    return t + pltpu.roll(t, D_ROPE, 1)


def _resident(shape):
    nd = len(shape)
    return pl.BlockSpec(shape, lambda *_: (0,) * nd, pipeline_mode=pl.Buffered(1))


def _params(sem):
    return pltpu.CompilerParams(dimension_semantics=sem, vmem_limit_bytes=VMEM_LIMIT_BYTES)


def _gmlp_kernel(n_p_tiles, n_x, *refs):
    x_refs = refs[:n_x]
    ng_ref, win_ref, bin_ref, sg_ref, ws_ref, bs_ref, wout_ref, bout_ref = refs[n_x:n_x + 8]
    xo_ref, v_ref, zu_ref, vb_ref, gated_ref = refs[n_x + 8:]
    i = pl.program_id(0)
    x = x_refs[0][...] if n_x == 1 else jnp.where(i < n_p_tiles, x_refs[0][...], x_refs[1][...])
    h = _rms(x, ng_ref[...]).astype(BF16)
    dg = sg_ref.shape[1]
    ssq = jnp.zeros((x.shape[0], 1), F32)
    for c0 in range(0, dg, GMLP_COL_CHUNK):
        cols = slice(c0, c0 + GMLP_COL_CHUNK)
        vcols = slice(dg + c0, dg + c0 + GMLP_COL_CHUNK)
        zv = jax.nn.gelu(_dot(h, win_ref[:, vcols]) + bin_ref[:, vcols])
        ssq = ssq + jnp.sum(zv * zv, axis=1, keepdims=True)
        v_ref[:, cols] = zv
        zu_ref[:, cols] = jax.nn.gelu(_dot(h, win_ref[:, cols]) + bin_ref[:, cols])
    v = v_ref[...] * lax.rsqrt(ssq * (1.0 / dg) + EPS) * sg_ref[...]
    vb_ref[...] = v.astype(BF16)
    v_ref[...] = v

    gw = dg // N_GROUPS
    for c in range(x.shape[0] // CHUNK):
        rows = slice(c * CHUNK, (c + 1) * CHUNK)
        for g in range(N_GROUPS):
            cols = slice(g * gw, (g + 1) * gw)
            mix = _dot(ws_ref[0, g], vb_ref[rows, cols]) + bs_ref[0, :, cols]
            gated_ref[rows, cols] = (zu_ref[rows, cols] * mix).astype(BF16)
    xo_ref[...] = x + _dot(gated_ref[...], wout_ref[...]) + bout_ref[...]


def _gmlp_layer(xs, n_p_tiles, tm, ng, w_in, b_in, sg, ws_eff, bs_eff, w_out, b_out):
    d = xs[0].shape[1]
    t = sum(a.shape[0] for a in xs)
    dg = sg.shape[1]
    n_tiles = t // tm
    t_s = t - n_p_tiles * tm
    sel = lambda i: jnp.where(i >= n_p_tiles, 1, 0)
    if len(xs) == 1:
        x_specs = [pl.BlockSpec((tm, d), lambda i: (i, 0))]
    else:
        x_specs = [pl.BlockSpec((tm, d), lambda i: (jnp.minimum(i, n_p_tiles - 1), 0)),
                   pl.BlockSpec((tm, d), lambda i: (jnp.maximum(i - n_p_tiles, 0), 0))]
    return pl.pallas_call(
        functools.partial(_gmlp_kernel, n_p_tiles, len(xs)),
        grid=(n_tiles,),
        in_specs=x_specs + [
            _resident(ng.shape), _resident(w_in.shape), _resident(b_in.shape), _resident(sg.shape),
            pl.BlockSpec((1,) + ws_eff.shape[1:], lambda i: (sel(i), 0, 0, 0)),
            pl.BlockSpec((1,) + bs_eff.shape[1:], lambda i: (sel(i), 0, 0)),
            _resident(w_out.shape), _resident(b_out.shape),
        ],
        out_specs=[
            pl.BlockSpec((tm, d), lambda i: (i, 0)),
            pl.BlockSpec((tm, dg), lambda i: (jnp.maximum(i - n_p_tiles, 0), 0)),
        ],
        out_shape=[jax.ShapeDtypeStruct((t, d), F32), jax.ShapeDtypeStruct((t_s, dg), F32)],
        scratch_shapes=[pltpu.VMEM((tm, dg), F32), pltpu.VMEM((tm, dg), BF16), pltpu.VMEM((tm, dg), BF16)],
        compiler_params=_params(("arbitrary",)),
        name="gmlp_layer",
    )(*xs, ng, w_in, b_in, sg, ws_eff, bs_eff, w_out, b_out)


def _ffn_kernel(x_ref, ng_ref, wg_ref, wu_ref, wd_ref, xo_ref):
    x = x_ref[...]
    h = _rms(x, ng_ref[...]).astype(BF16)
    t = (jax.nn.silu(_dot(h, wg_ref[...])) * _dot(h, wu_ref[...])).astype(BF16)
    xo_ref[...] = x + _dot(t, wd_ref[...])


def _ffn_dense(x, tm, ng, wg, wu, wd):
    t, d = x.shape
    return pl.pallas_call(
        _ffn_kernel,
        grid=(t // tm,),
        in_specs=[pl.BlockSpec((tm, d), lambda i: (i, 0)),
                  _resident(ng.shape), _resident(wg.shape), _resident(wu.shape), _resident(wd.shape)],
        out_specs=pl.BlockSpec((tm, d), lambda i: (i, 0)),
        out_shape=jax.ShapeDtypeStruct((t, d), F32),
        compiler_params=_params(("arbitrary",)),
        name="ffn_dense",
    )(x, ng, wg, wu, wd)


MOE_ROW_TILE = 512
DISPATCH_TOKENS = 1024
COMBINE_TOKENS = 256
SAMPLE_SUB_PAGES = 8


def _router_kernel(x_ref, ng_ref, wrt_ref, tri_ref, idx_ref, gate_ref, rank_ref, cnt_ref, run_ref):
    @pl.when(pl.program_id(0) == 0)
    def _():
        run_ref[...] = jnp.zeros_like(run_ref)

    hb = _rms(x_ref[...], ng_ref[...]).astype(BF16)
    logits = _dot_nt(wrt_ref[...], hb)
    ne = logits.shape[0]
    ex = jnp.exp(logits - jnp.max(logits, axis=0, keepdims=True))
    probs = ex / jnp.sum(ex, axis=0, keepdims=True)
    row = lax.broadcasted_iota(jnp.int32, probs.shape, 0)
    m1 = jnp.max(probs, axis=0, keepdims=True)
    i1 = jnp.min(jnp.where(probs == m1, row, ne), axis=0, keepdims=True)
    rest = jnp.where(row == i1, -1.0, probs)
    m2 = jnp.max(rest, axis=0, keepdims=True)
    i2 = jnp.min(jnp.where(rest == m2, row, ne), axis=0, keepdims=True)
    den = m1 + m2
    oh1 = row == i1
    oh2 = row == i2
    oh = jnp.where(oh1 | oh2, 1.0, 0.0)
    rank = _dot(oh.astype(BF16), tri_ref[...]) + run_ref[:, :1]
    idx_ref[...] = jnp.concatenate([i1, i2], axis=0)
    gate_ref[...] = jnp.concatenate([m1 / den, m2 / den], axis=0)
    rank_ref[...] = jnp.concatenate([jnp.sum(jnp.where(oh1, rank, 0.0), axis=0, keepdims=True),
                                     jnp.sum(jnp.where(oh2, rank, 0.0), axis=0, keepdims=True)],
                                    axis=0).astype(jnp.int32)
    run = run_ref[...] + jnp.sum(oh, axis=1, keepdims=True)
    run_ref[...] = run
    cnt_ref[...] = run.astype(jnp.int32)


def _router(x, tm, ng, wrt):
    t, d = x.shape
    ne = wrt.shape[0]
    tri = jnp.triu(jnp.ones((tm, tm), BF16), k=1)
    tok = lambda i: (0, i)
    return pl.pallas_call(
        _router_kernel,
        grid=(t // tm,),
        in_specs=[pl.BlockSpec((tm, d), lambda i: (i, 0)), _resident(ng.shape), _resident(wrt.shape),
                  _resident(tri.shape)],
        out_specs=[pl.BlockSpec((TOP_K, tm), tok), pl.BlockSpec((TOP_K, tm), tok), pl.BlockSpec((TOP_K, tm), tok),
                   pl.BlockSpec((ne, LANES), lambda i: (0, 0))],
        out_shape=[jax.ShapeDtypeStruct((TOP_K, t), jnp.int32), jax.ShapeDtypeStruct((TOP_K, t), F32),
                   jax.ShapeDtypeStruct((TOP_K, t), jnp.int32), jax.ShapeDtypeStruct((ne, LANES), jnp.int32)],
        scratch_shapes=[pltpu.VMEM((ne, LANES), F32)],
        compiler_params=_params(("arbitrary",)),
        name="moe_router",
    )(x, ng, wrt, tri)


def _row_copy(src_ref, src_row, dst_ref, dst_row, sem):
    return pltpu.make_async_copy(src_ref.at[pl.ds(src_row, 1)], dst_ref.at[pl.ds(dst_row, 1)], sem)


def _dispatch_kernel(fill_lo_ref, fill_hi_ref, pos_ref, x_ref, xs_hbm, zero_ref, sem, zsem):
    n = pos_ref.shape[1]

    @pl.when(pl.program_id(0) == 0)
    def _():
        zero_ref[...] = jnp.zeros_like(zero_ref)

        def fill(row, carry):
            _row_copy(zero_ref, 0, xs_hbm, row, zsem).start()
            return carry

        def drain(row, carry):
            _row_copy(zero_ref, 0, xs_hbm, row, zsem).wait()
            return carry

        for e in range(fill_lo_ref.shape[0]):
            lax.fori_loop(fill_lo_ref[e], fill_hi_ref[e], fill, 0)
        for e in range(fill_lo_ref.shape[0]):
            lax.fori_loop(fill_lo_ref[e], fill_hi_ref[e], drain, 0)

    def issue(r, carry):
        for k in range(TOP_K):
            _row_copy(x_ref, r, xs_hbm, pos_ref[k, r], sem).start()
        return carry

    lax.fori_loop(0, n, issue, 0, unroll=8)
    for k in range(TOP_K):
        pltpu.make_async_copy(x_ref, xs_hbm.at[pl.ds(0, n)], sem).wait()


def _dispatch(x, pos, fill_lo, fill_hi, n_rows):
    t, d = x.shape
    n = DISPATCH_TOKENS if t % DISPATCH_TOKENS == 0 else LANES
    return pl.pallas_call(
        _dispatch_kernel,
        grid_spec=pltpu.PrefetchScalarGridSpec(
            num_scalar_prefetch=2,
            grid=(t // n,),
            in_specs=[pl.BlockSpec((TOP_K, n), lambda i, lo, hi: (0, i), memory_space=pltpu.SMEM),
                      pl.BlockSpec((n, d), lambda i, lo, hi: (i, 0))],
            out_specs=pl.BlockSpec(memory_space=pl.ANY),
            scratch_shapes=[pltpu.VMEM((8, d), F32), pltpu.SemaphoreType.DMA(()), pltpu.SemaphoreType.DMA(())]),
        out_shape=jax.ShapeDtypeStruct((n_rows, d), F32),
        compiler_params=_params(("arbitrary",)),
        name="moe_dispatch",
    )(fill_lo, fill_hi, pos, x)


def _grouped_ffn_kernel(layer, te_ref, nu_ref, xs_ref, ng_ref, wg_ref, wu_ref, wd_ref, ys_ref):
    del layer, te_ref
    i = pl.program_id(0)

    @pl.when(i < nu_ref[0])
    def _():
        h = _rms(xs_ref[...], ng_ref[...]).astype(BF16)
        t = (jax.nn.silu(_dot(h, wg_ref[0, 0])) * _dot(h, wu_ref[0, 0])).astype(BF16)
        ys_ref[...] = _dot(t, wd_ref[0, 0])

    @pl.when(i >= nu_ref[0])
    def _():
        ys_ref[...] = jnp.zeros_like(ys_ref)


def _grouped_ffn(xs, tile_expert, n_used, layer, ng, wg, wu, wd):
    n_rows, d = xs.shape
    dff = wg.shape[3]
    tmg = MOE_ROW_TILE
    wspec = lambda shape: pl.BlockSpec((1, 1) + shape, lambda i, te, nu: (layer, te[i], 0, 0),
                                       pipeline_mode=pl.Buffered(1))
    return pl.pallas_call(
        functools.partial(_grouped_ffn_kernel, layer),
        grid_spec=pltpu.PrefetchScalarGridSpec(
            num_scalar_prefetch=2,
            grid=(n_rows // tmg,),
            in_specs=[pl.BlockSpec((tmg, d), lambda i, te, nu: (jnp.minimum(i, jnp.maximum(nu[0], 1) - 1), 0)),
                      pl.BlockSpec(ng.shape, lambda i, te, nu: (0, 0), pipeline_mode=pl.Buffered(1)),
                      wspec((d, dff)), wspec((d, dff)), wspec((dff, d))],
            out_specs=pl.BlockSpec((tmg, d), lambda i, te, nu: (i, 0))),
        out_shape=jax.ShapeDtypeStruct((n_rows, d), F32),
        compiler_params=_params(("arbitrary",)),
        name="moe_grouped_ffn",
    )(tile_expert, n_used, xs, ng, wg, wu, wd)


def _combine_kernel(n_first, pos_ref, pos_next_ref, x_ref, gate_ref, ys_hbm, *refs):
    outs, (buf_ref, sem) = refs[:-2], refs[-2:]
    n = x_ref.shape[0]
    i = pl.program_id(0)
    slot = i % 2

    def gather(p_ref, s):
        def issue(r, carry):
            for k in range(TOP_K):
                _row_copy(ys_hbm, p_ref[k, r], buf_ref.at[s, k], r, sem.at[s]).start()
            return carry
        lax.fori_loop(0, n, issue, 0, unroll=8)

    @pl.when(i == 0)
    def _():
        gather(pos_ref, 0)

    @pl.when(i + 1 < pl.num_programs(0))
    def _():
        gather(pos_next_ref, 1 - slot)

    for k in range(TOP_K):
        pltpu.make_async_copy(ys_hbm.at[pl.ds(0, n)], buf_ref.at[slot, k], sem.at[slot]).wait()
    g = gate_ref[...]
    y = x_ref[...] + (g[:, 0:1] * buf_ref[slot, 0] + g[:, 1:2] * buf_ref[slot, 1])
    if len(outs) == 1:
        outs[0][...] = y
    else:
        @pl.when(i < n_first)
        def _():
            outs[0][...] = y

        @pl.when(i >= n_first)
        def _():
            outs[1][...] = y


def _combine(x, pos, gates, ys, t_first=None):
    t, d = x.shape
    n = COMBINE_TOKENS if t % COMBINE_TOKENS == 0 else LANES
    n_steps = t // n
    if t_first is None:
        n_first = n_steps
        out_specs = pl.BlockSpec((n, d), lambda i: (i, 0))
        out_shape = jax.ShapeDtypeStruct((t, d), F32)
    else:
        assert t_first % n == 0 and 0 < t_first < t
        n_first = t_first // n
        out_specs = [pl.BlockSpec((n, d), lambda i: (jnp.minimum(i, n_first - 1), 0)),
                     pl.BlockSpec((n, d), lambda i: (jnp.maximum(i - n_first, 0), 0))]
        out_shape = [jax.ShapeDtypeStruct((t_first, d), F32), jax.ShapeDtypeStruct((t - t_first, d), F32)]
    return pl.pallas_call(
        functools.partial(_combine_kernel, n_first),
        grid=(n_steps,),
        in_specs=[pl.BlockSpec((TOP_K, n), lambda i: (0, i), memory_space=pltpu.SMEM),
                  pl.BlockSpec((TOP_K, n), lambda i: (0, jnp.minimum(i + 1, n_steps - 1)), memory_space=pltpu.SMEM),
                  pl.BlockSpec((n, d), lambda i: (i, 0)), pl.BlockSpec((n, TOP_K), lambda i: (i, 0)),
                  pl.BlockSpec(memory_space=pl.ANY)],
        out_specs=out_specs,
        out_shape=out_shape,
        scratch_shapes=[pltpu.VMEM((2, TOP_K, n, d), F32), pltpu.SemaphoreType.DMA((2,))],
        compiler_params=_params(("arbitrary",)),
        name="moe_combine",
    )(pos, pos, x, gates, ys)


def _moe_layer(x, tm, layer, ng, wrt, wg, wu, wd, t_first=None):
    t, d = x.shape
    ne = wrt.shape[0]
    tmg = MOE_ROW_TILE
    n_rows = -(-(t * TOP_K) // tmg) * tmg + ne * tmg
    idx, gate, rank, cnt = _router(x, tm, ng, wrt)
    padded = (cnt[:, 0] + tmg - 1) // tmg * tmg
    ends = jnp.cumsum(padded)
    starts = ends - padded
    pos = rank + sum(jnp.where(idx == e, starts[e], 0) for e in range(ne))
    tile_start = jnp.arange(n_rows // tmg, dtype=jnp.int32) * tmg
    tile_expert = jnp.minimum(jnp.sum(tile_start[:, None] >= ends[None, :], axis=1), ne - 1).astype(jnp.int32)
    n_used = (ends[-1:] // tmg).astype(jnp.int32)
    fill_hi = jnp.where(jnp.arange(ne) == ne - 1, n_rows, ends).astype(jnp.int32)
    xs = _dispatch(x, pos, starts + cnt[:, 0], fill_hi, n_rows)
    ys = _grouped_ffn(xs, tile_expert, n_used, layer, ng, wg, wu, wd)
    return _combine(x, pos, gate.T, ys, t_first)


def _keyside_kernel(x_ref, ng_ref, wdkv_ref, lg_ref, wkr_ref, krg_ref, cs_ref, wukt_ref,
                    c_ref, kr_ref, cb_ref, krp_ref, rt_ref):
    h = _rms(x_ref[...], ng_ref[...]).astype(BF16)
    c = _rms(_dot(h, wdkv_ref[...]), lg_ref[...])
    cb = c.astype(BF16)
    c_ref[...] = c
    cb_ref[...] = cb
    full = _rope_pairs(_dot(h, wkr_ref[...]), krg_ref[...], cs_ref[...])
    kr_ref[...] = full[:, :D_ROPE]
    lane = lax.broadcasted_iota(jnp.int32, full.shape, 1)
    krp_ref[...] = jnp.where(lane < D_ROPE, full, 0.0).astype(BF16)
    rt_ref[...] = _key_rms_scale_t(wukt_ref[...], cb)


def _key_rms_scale_t(wukt, cb):
    kt = _dot_nt(wukt, cb)
    ssq = jnp.sum((kt * kt).reshape(N_HEADS, D_NOPE, kt.shape[1]), axis=1)
    return lax.rsqrt(ssq * (1.0 / D_NOPE) + EPS)


def _key_side(x, tm, cs_map, ng, w_dkv, lg, w_kr2, krg2, cs_tab, wukt):
    t, d = x.shape
    r = w_dkv.shape[1]
    row = lambda i: (i, 0)
    return pl.pallas_call(
        _keyside_kernel,
        grid=(t // tm,),
        in_specs=[pl.BlockSpec((tm, d), row),
                  _resident(ng.shape), _resident(w_dkv.shape), _resident(lg.shape),
                  _resident(w_kr2.shape), _resident(krg2.shape),
                  pl.BlockSpec((tm, LANES), lambda i: (cs_map(i), 0)),
                  _resident(wukt.shape)],
        out_specs=[pl.BlockSpec((tm, r), row), pl.BlockSpec((tm, D_ROPE), row), pl.BlockSpec((tm, r), row),
                   pl.BlockSpec((tm, LANES), row), pl.BlockSpec((N_HEADS, tm), lambda i: (0, i))],
        out_shape=[jax.ShapeDtypeStruct((t, r), F32), jax.ShapeDtypeStruct((t, D_ROPE), F32),
                   jax.ShapeDtypeStruct((t, r), BF16), jax.ShapeDtypeStruct((t, LANES), BF16),
                   jax.ShapeDtypeStruct((N_HEADS, t), F32)],
        compiler_params=_params(("arbitrary",)),
        name="key_side",
    )(x, ng, w_dkv, lg, w_kr2, krg2, cs_tab, wukt)


def _paged_fetch(pt_ref, n_pages, n_pg, caches, bufs, sems):
    n_steps = pl.num_programs(1)
    t = pl.program_id(0) * n_steps + pl.program_id(1)
    slot = t % 2

    def start(step, s):
        base = (step // n_steps) * n_pages + (step % n_steps) * n_pg
        for g in range(n_pg):
            pid = pt_ref[base + g]
            for cache, buf, sem in zip(caches, bufs, sems):
                pltpu.make_async_copy(cache.at[pid], buf.at[s, g], sem.at[s]).start()

    @pl.when(t == 0)
    def _():
        start(t, 0)

    @pl.when(t + 1 < pl.num_programs(0) * n_steps)
    def _():
        start(t + 1, 1 - slot)

    for cache, buf, sem in zip(caches, bufs, sems):
        pltpu.make_async_copy(cache.at[pl.ds(0, n_pg)], buf.at[slot], sem.at[slot]).wait()
    return slot


def _past_scale_kernel(n_pages, n_pg, pt_ref, wukt_ref, cache_hbm, rt_ref, cbuf_ref, sem):
    slot = _paged_fetch(pt_ref, n_pages, n_pg, [cache_hbm], [cbuf_ref], [sem])
    cb = jnp.concatenate([cbuf_ref[slot, g].astype(BF16) for g in range(n_pg)], axis=0)
    rt_ref[0] = _key_rms_scale_t(wukt_ref[...], cb)


def _past_scale(page_table, cache_latent, wukt, n_pg):
    n_seq, n_pages = page_table.shape
    page, r = cache_latent.shape[1:]
    gk = n_pg * page
    return pl.pallas_call(
        functools.partial(_past_scale_kernel, n_pages, n_pg),
        grid_spec=pltpu.PrefetchScalarGridSpec(
            num_scalar_prefetch=1,
            grid=(n_seq, n_pages // n_pg),
            in_specs=[pl.BlockSpec(wukt.shape, lambda b, j, pt: (0, 0), pipeline_mode=pl.Buffered(1)),
                      pl.BlockSpec(memory_space=pl.ANY)],
            out_specs=pl.BlockSpec((1, N_HEADS, gk), lambda b, j, pt: (b, 0, j)),
            scratch_shapes=[pltpu.VMEM((2, n_pg, page, r), F32), pltpu.SemaphoreType.DMA((2,))]),
        out_shape=jax.ShapeDtypeStruct((n_seq, N_HEADS, n_pages * page), F32),
        compiler_params=_params(("arbitrary", "arbitrary")),
        name="past_key_scale",
    )(page_table.reshape(-1), wukt, cache_latent)


def _qside_kernel(x_ref, ng_ref, wdq_ref, qlg_ref, wn_ref, wr_ref, qng_ref, kng_ref, qrg_ref, cs_ref, wukt_ref,
                  qa_ref, qr_ref):
    h = _rms(x_ref[...], ng_ref[...]).astype(BF16)
    cq = _rms(_dot(h, wdq_ref[...]), qlg_ref[...]).astype(BF16)
    qn_all = _dot(cq, wn_ref[...])
    qr_all = _dot(cq, wr_ref[...])
    cs = cs_ref[...]
    for hh in range(N_HEADS):
        cols = slice(hh * LANES, (hh + 1) * LANES)
        qn = (_rms(qn_all[:, cols], qng_ref[...]) * kng_ref[...]).astype(BF16)
        qa_ref[hh] = _dot(qn, wukt_ref[hh]).astype(BF16)
        qr_ref[hh] = _rope_pairs(qr_all[:, cols], qrg_ref[...], cs).astype(BF16)


def _q_side(x, tm, cs_map, ng, w_dq, qlg, wn, wr2, qng, kng, qrg2, cs_tab, wukt3):
    t, d = x.shape
    r = wukt3.shape[2]
    return pl.pallas_call(
        _qside_kernel,
        grid=(t // tm,),
        in_specs=[pl.BlockSpec((tm, d), lambda i: (i, 0)),
                  _resident(ng.shape), _resident(w_dq.shape), _resident(qlg.shape), _resident(wn.shape),
                  _resident(wr2.shape), _resident(qng.shape), _resident(kng.shape), _resident(qrg2.shape),
                  pl.BlockSpec((tm, LANES), lambda i: (cs_map(i), 0)),
                  _resident(wukt3.shape)],
        out_specs=[pl.BlockSpec((N_HEADS, tm, r), lambda i: (0, i, 0)),
                   pl.BlockSpec((N_HEADS, tm, LANES), lambda i: (0, i, 0))],
        out_shape=[jax.ShapeDtypeStruct((N_HEADS, t, r), BF16), jax.ShapeDtypeStruct((N_HEADS, t, LANES), BF16)],
        compiler_params=_params(("arbitrary",)),
        name="q_side",
    )(x, ng, w_dq, qlg, wn, wr2, qng, kng, qrg2, cs_tab, wukt3)


def _softmax_step(s, cb, m, l, acc):
    m_new = jnp.maximum(m, jnp.max(s, axis=1, keepdims=True))
    alpha = jnp.exp(m - m_new)
    p = jnp.exp(s - m_new)
    l_new = alpha * l + jnp.sum(p, axis=1, keepdims=True)
    acc_new = alpha * acc + _dot(p.astype(BF16), cb)
    return m_new, l_new, acc_new


def _scores(s1, s2, rt, scale):
    nrow, nk = s1.shape
    s1 = s1.reshape(N_HEADS, nrow // N_HEADS, nk) * rt[:, None, :]
    return (s1.reshape(nrow, nk) + s2) * scale


def _prompt_attn_kernel(bq, bk, scale, qa_ref, qr_ref, cb_ref, krp_ref, rt_ref, o_ref, m_ref, acc_ref):
    i = pl.program_id(1)
    r = qa_ref.shape[2]
    q = qa_ref[...].reshape(N_HEADS * bq, r)
    qr = qr_ref[...].reshape(N_HEADS * bq, LANES)
    m_ref[...] = jnp.full_like(m_ref, -jnp.inf)
    acc_ref[...] = jnp.zeros_like(acc_ref)
    def step(k0, w, masked):
        cb = cb_ref[pl.ds(k0, w), :]
        rts = rt_ref[:, pl.ds(k0, w)] * scale
        s = _dot_nt(q, cb).reshape(N_HEADS, bq, w) * rts[:, None, :]
        s = s + (_dot_nt(qr, krp_ref[pl.ds(k0, w), :]) * scale).reshape(N_HEADS, bq, w)
        if masked:
            qpos = i * bq + lax.broadcasted_iota(jnp.int32, (1, bq, w), 1)
            kpos = k0 + lax.broadcasted_iota(jnp.int32, (1, bq, w), 2)
            s = jnp.where(kpos <= qpos, s, -jnp.inf)
        s = s.reshape(N_HEADS * bq, w)
        m_prev = m_ref[...]
        m_new = jnp.maximum(m_prev, jnp.max(s, axis=1, keepdims=True))
        alpha = jnp.exp(m_prev - m_new)
        p = jnp.exp(s - jnp.tile(m_new, (1, w // LANES))).astype(BF16)
        m_ref[...] = m_new
        acc_ref[...] = (jnp.tile(alpha, (1, acc_ref.shape[1] // LANES)) * acc_ref[...]
                        + _dot(p, jnp.concatenate([cb, jnp.ones((w, LANES), BF16)], axis=1)))

    def full_block(j, carry):
        step(pl.multiple_of(j * bk, bk), bk, False)
        return carry

    n_visible = (i * bq) // bk
    lax.fori_loop(0, n_visible, full_block, 0)
    k_diag = pl.multiple_of(n_visible * bk, bk)
    for part in range(bk // bq):
        @pl.when(i % (bk // bq) == part)
        def _():
            step(k_diag, (part + 1) * bq, True)
    denom = jnp.tile(acc_ref[:, r:], (1, r // LANES))
    o_ref[...] = (acc_ref[:, :r] / denom).reshape(N_HEADS, bq, r).astype(BF16)


def _prompt_attention(qa, qr, cb, krp, rt, bsz, seq, bq, bk, scale):
    r = cb.shape[1]
    nq = seq // bq
    nrow = N_HEADS * bq
    qmap = lambda b, i: (0, b * nq + i, 0)
    return pl.pallas_call(
        functools.partial(_prompt_attn_kernel, bq, bk, scale),
        grid=(bsz, nq),
        in_specs=[pl.BlockSpec((N_HEADS, bq, r), qmap), pl.BlockSpec((N_HEADS, bq, LANES), qmap),
                  pl.BlockSpec((seq, r), lambda b, i: (b, 0)), pl.BlockSpec((seq, LANES), lambda b, i: (b, 0)),
                  pl.BlockSpec((N_HEADS, seq), lambda b, i: (0, b))],
        out_specs=pl.BlockSpec((N_HEADS, bq, r), qmap),
        out_shape=jax.ShapeDtypeStruct((N_HEADS, bsz * seq, r), BF16),
        scratch_shapes=[pltpu.VMEM((nrow, LANES), F32), pltpu.VMEM((nrow, r + LANES), F32)],
        compiler_params=_params(("arbitrary", "arbitrary")),
        name="prompt_attention",
    )(qa, qr, cb, krp, rt)


def _sample_attn_kernel(n_pages, n_pg, n_new, scale, pt_ref, qa_ref, qr_ref, rtp_ref, cn_ref, krn_ref, rtn_ref,
                        cache_hbm, krope_hbm, o_ref, m_ref, l_ref, acc_ref, cbuf_ref, kbuf_ref, csem, ksem):
    j = pl.program_id(1)
    slot = _paged_fetch(pt_ref, n_pages, n_pg, [cache_hbm, krope_hbm], [cbuf_ref, kbuf_ref], [csem, ksem])

    @pl.when(j == 0)
    def _():
        m_ref[...] = jnp.full_like(m_ref, -jnp.inf)
        l_ref[...] = jnp.zeros_like(l_ref)
        acc_ref[...] = jnp.zeros_like(acc_ref)

    q = qa_ref[0]
    qr = qr_ref[0][:, :D_ROPE]
    sub = min(n_pg, SAMPLE_SUB_PAGES)
    parts = []
    for g0 in range(0, n_pg, sub):
        cb = jnp.concatenate([cbuf_ref[slot, g].astype(BF16) for g in range(g0, g0 + sub)], axis=0)
        kb = jnp.concatenate([kbuf_ref[slot, g].astype(BF16) for g in range(g0, g0 + sub)], axis=1)
        s = _scores(_dot_nt(q, cb), _dot(qr, kb), rtp_ref[0, :, g0 * CHUNK:(g0 + sub) * CHUNK], scale)
        mj = jnp.max(s, axis=1, keepdims=True)
        p = jnp.exp(s - mj)
        parts.append((mj, jnp.sum(p, axis=1, keepdims=True), _dot(p.astype(BF16), cb)))
    m_prev = m_ref[...]
    m = functools.reduce(jnp.maximum, [mj for mj, _, _ in parts], m_prev)
    alpha = jnp.exp(m_prev - m)
    l = alpha * l_ref[...]
    acc = alpha * acc_ref[...]
    for mj, lj, accj in parts:
        aj = jnp.exp(mj - m)
        l = l + aj * lj
        acc = acc + aj * accj
    m_ref[...] = m
    l_ref[...] = l
    acc_ref[...] = acc

    @pl.when(j == pl.num_programs(1) - 1)
    def _():
        pad = CHUNK - n_new
        cn = jnp.concatenate([cn_ref[0], jnp.zeros((pad, cn_ref.shape[2]), F32)], axis=0).astype(BF16)
        krn = jnp.concatenate([krn_ref[0], jnp.zeros((pad, D_ROPE), F32)], axis=0).astype(BF16)
        sn = _scores(_dot_nt(q, cn), _dot_nt(qr, krn), rtn_ref[0], scale)
        nrow = q.shape[0]
        qi = lax.broadcasted_iota(jnp.int32, (N_HEADS, nrow // N_HEADS, CHUNK), 1).reshape(nrow, CHUNK)
        ki = lax.broadcasted_iota(jnp.int32, (nrow, CHUNK), 1)
        sn = jnp.where(ki <= qi, sn, -jnp.inf)
        _, l2, acc2 = _softmax_step(sn, cn, m, l, acc)
        o_ref[0] = (acc2 / l2).astype(BF16)


def _sample_attention(page_table, qa_s, qr_s, rt_past, c_new, kr_new, rt_new, cache_latent, cache_krope_t,
                      n_pg, scale):
    n_seq, n_pages = page_table.shape
    page, r = cache_latent.shape[1:]
    nrow = qa_s.shape[1]
    n_new = c_new.shape[1]
    gk = n_pg * page
    seq_blk = lambda shape: pl.BlockSpec((1,) + shape, lambda b, j, pt: (b, 0, 0))
    any_spec = pl.BlockSpec(memory_space=pl.ANY)
    return pl.pallas_call(
        functools.partial(_sample_attn_kernel, n_pages, n_pg, n_new, scale),
        grid_spec=pltpu.PrefetchScalarGridSpec(
            num_scalar_prefetch=1,
            grid=(n_seq, n_pages // n_pg),
            in_specs=[seq_blk((nrow, r)), seq_blk((nrow, LANES)),
                      pl.BlockSpec((1, N_HEADS, gk), lambda b, j, pt: (b, 0, j)),
                      seq_blk((n_new, r)), seq_blk((n_new, D_ROPE)), seq_blk((N_HEADS, LANES)),
                      any_spec, any_spec],
            out_specs=seq_blk((nrow, r)),
            scratch_shapes=[pltpu.VMEM((nrow, 1), F32), pltpu.VMEM((nrow, 1), F32), pltpu.VMEM((nrow, r), F32),
                            pltpu.VMEM((2, n_pg, page, r), F32), pltpu.VMEM((2, n_pg, D_ROPE, page), F32),
                            pltpu.SemaphoreType.DMA((2,)), pltpu.SemaphoreType.DMA((2,))]),
        out_shape=jax.ShapeDtypeStruct((n_seq, nrow, r), BF16),
        compiler_params=_params(("arbitrary", "arbitrary")),
        name="sample_attention",
    )(page_table.reshape(-1), qa_s, qr_s, rt_past, c_new, kr_new, rt_new, cache_latent, cache_krope_t)


def _attn_out_kernel(n_p_tiles, x_ref, op_ref, os_ref, wuv_ref, wo_ref, xo_ref):
    i = pl.program_id(0)

    def project(o_ref):
        o = jnp.concatenate([_dot(o_ref[hh], wuv_ref[hh]) for hh in range(N_HEADS)], axis=1).astype(BF16)
        xo_ref[...] = x_ref[...] + _dot(o, wo_ref[...])

    @pl.when(i < n_p_tiles)
    def _():
        project(op_ref)

    @pl.when(i >= n_p_tiles)
    def _():
        project(os_ref)


def _attn_out(x, o_p, o_s, n_p_tiles, tm, w_uv3, w_o):
    t, d = x.shape
    r = o_p.shape[2]
    return pl.pallas_call(
        functools.partial(_attn_out_kernel, n_p_tiles),
        grid=(t // tm,),
        in_specs=[pl.BlockSpec((tm, d), lambda i: (i, 0)),
                  pl.BlockSpec((N_HEADS, tm, r), lambda i: (0, jnp.minimum(i, n_p_tiles - 1), 0)),
                  pl.BlockSpec((N_HEADS, tm, r), lambda i: (0, jnp.maximum(i - n_p_tiles, 0), 0)),
                  _resident(w_uv3.shape), _resident(w_o.shape)],
        out_specs=pl.BlockSpec((tm, d), lambda i: (i, 0)),
        out_shape=jax.ShapeDtypeStruct((t, d), F32),
        compiler_params=_params(("arbitrary",)),
        name="attn_out",
    )(x, o_p, o_s, w_uv3, w_o)


def _swap_halves(a):
    half = a.shape[-1] // 2
    return jnp.concatenate([a[..., half:], a[..., :half]], axis=-1)


def _rope_table(pos):
    half = D_ROPE // 2
    inv = ROPE_THETA ** (-jnp.arange(half, dtype=F32) / half)
    ang = pos.astype(F32)[:, None] * inv[None, :]
    cos, sin = jnp.cos(ang), jnp.sin(ang)
    return jnp.concatenate([cos, cos, -sin, sin], axis=1)


def _spatial_weights(w_s, b_s, length, gw):
    reps = CHUNK // length
    w = w_s[:, :length, :length] * jnp.tril(jnp.ones((length, length), w_s.dtype))
    w_eff = jnp.einsum("st,gpq->gsptq", jnp.eye(reps, dtype=w.dtype), w).reshape(N_GROUPS, CHUNK, CHUNK)
    b_eff = jnp.repeat(jnp.tile(b_s[:, :length].T, (reps, 1)), gw, axis=1)
    return w_eff, b_eff


def kernel(x_prompt, x_sample, cache_latent, cache_krope, page_table, a_norm_g, a_w_in, a_b_in, a_sgu_g, a_w_s, a_b_s, a_w_out, a_b_out, kv_norm_g, w_dkv, kv_lat_g, w_kr, kr_g, kn_g, w_uk, w_uv, b_norm_g, b_w_dq, b_q_lat_g, b_w_uq, b_qn_g, b_qr_g, b_w_o, f_norm_g, d_w_gate, d_w_up, d_w_down, m_w_router, m_w_gate, m_w_up, m_w_down):
    bsz, seq, d = x_prompt.shape
    n_seq, dec = x_sample.shape[:2]
    t_p, t_s = bsz * seq, n_seq * dec
    n_pages = page_table.shape[1]
    past_len = n_pages * cache_latent.shape[1]
    depth = f_norm_g.shape[0]
    n_a = a_w_in.shape[0]
    dg = a_sgu_g.shape[1]
    kv_rank = w_dkv.shape[1]
    scale = float((D_NOPE + D_ROPE) ** -0.5)

    tm = 512 if t_s % 512 == 0 else CHUNK
    assert t_p % tm == 0 and t_s % tm == 0 and seq % tm == 0 and tm % CHUNK == 0
    assert seq % CHUNK == 0 and CHUNK % dec == 0 and dec <= CHUNK and tm % dec == 0
    n_p_tiles = t_p // tm
    n_pg_attn = next(g for g in (64, 32, 16, 8, 4, 2, 1) if n_pages % g == 0)
    n_pg_scale = next(g for g in (32, 16, 8, 4, 2, 1) if n_pages % g == 0)
    kv_block = next(k for k in (512, 256, CHUNK) if seq % k == 0)
    row = lambda a: a.reshape(1, -1)
    bf = lambda a: a.astype(BF16)

    assert n_a >= 1
    x = None

    pos_p = jnp.arange(seq, dtype=jnp.int32)
    pos_s = past_len + jnp.arange(dec, dtype=jnp.int32)
    cs_tab = jnp.concatenate([_rope_table(pos_p), jnp.tile(_rope_table(pos_s), (tm // dec, 1))], axis=0)
    tiles_per_seq = seq // tm
    cs_map = lambda i: jnp.where(i < n_p_tiles, i % tiles_per_seq, tiles_per_seq)

    cache_krope_t = jnp.swapaxes(cache_krope, 1, 2)
    wukt3 = bf(jnp.transpose(w_uk, (1, 2, 0)))
    wukt = wukt3.reshape(N_HEADS * D_NOPE, kv_rank)
    w_uv3 = bf(jnp.transpose(w_uv, (1, 0, 2)))
    m_wg, m_wu, m_wd = bf(m_w_gate), bf(m_w_up), bf(m_w_down)

    v_rows = []
    c = kr = cb = krp = rt = rt_past = None
    for layer in range(depth):
        if layer < n_a:
            a = layer
            ws_p, bs_p = _spatial_weights(a_w_s[a], a_b_s[a], min(seq, CHUNK), dg // N_GROUPS)
            ws_s, bs_s = _spatial_weights(a_w_s[a], a_b_s[a], min(dec, CHUNK), dg // N_GROUPS)
            x_in = [x_prompt.reshape(t_p, d), x_sample.reshape(t_s, d)] if layer == 0 else [x]
            x, v = _gmlp_layer(x_in, n_p_tiles, tm, row(a_norm_g[a]), bf(a_w_in[a]), row(a_b_in[a]), row(a_sgu_g[a]),
                               bf(jnp.stack([ws_p, ws_s])), jnp.stack([bs_p, bs_s]), bf(a_w_out[a]), row(a_b_out[a]))
            v_rows.append(v.reshape(n_seq, dec, dg))
        else:
            if layer == n_a:
                w_kr2 = bf(jnp.concatenate([w_kr, _swap_halves(w_kr)], axis=1))
                krg2 = row(jnp.concatenate([kr_g, _swap_halves(kr_g)]))
                c, kr, cb, krp, rt = _key_side(x, tm, cs_map, row(kv_norm_g), bf(w_dkv), row(kv_lat_g), w_kr2, krg2,
                                               cs_tab, wukt)
                rt_past = _past_scale(page_table, cache_latent, wukt, n_pg_scale)
                c_new = c[t_p:].reshape(n_seq, dec, kv_rank)
                kr_new = kr[t_p:].reshape(n_seq, dec, D_ROPE)
                rt_new = jnp.transpose(rt[:, t_p:].reshape(N_HEADS, n_seq, dec), (1, 0, 2))
                rt_new = jnp.pad(rt_new, ((0, 0), (0, 0), (0, LANES - dec)))
            b = layer - n_a
            w_uq = b_w_uq[b]
            wn = bf(w_uq[:, :, :D_NOPE].reshape(w_uq.shape[0], N_HEADS * D_NOPE))
            wr = w_uq[:, :, D_NOPE:]
            wr2 = bf(jnp.concatenate([wr, _swap_halves(wr)], axis=2).reshape(w_uq.shape[0], N_HEADS * LANES))
            qrg2 = row(jnp.concatenate([b_qr_g[b], _swap_halves(b_qr_g[b])]))
            qa, qr = _q_side(x, tm, cs_map, row(b_norm_g[b]), bf(b_w_dq[b]), row(b_q_lat_g[b]), wn, wr2,
                             row(b_qn_g[b]), row(kn_g), qrg2, cs_tab, wukt3)
            o_p = _prompt_attention(qa, qr, cb, krp, rt, bsz, seq, QBLOCK, kv_block, scale)

            def per_seq(a):
                w = a.shape[2]
                return jnp.transpose(a[:, t_p:].reshape(N_HEADS, n_seq, dec, w), (1, 0, 2, 3)).reshape(
                    n_seq, N_HEADS * dec, w)

            o_s = _sample_attention(page_table, per_seq(qa), per_seq(qr), rt_past, c_new, kr_new, rt_new,
                                    cache_latent, cache_krope_t, n_pg_attn, scale)
            o_s = jnp.transpose(o_s.reshape(n_seq, N_HEADS, dec, kv_rank), (1, 0, 2, 3)).reshape(
                N_HEADS, t_s, kv_rank)
            x = _attn_out(x, o_p, o_s, n_p_tiles, tm, w_uv3, bf(b_w_o[b]))
        i = layer // 2
        if layer % 2 == 0:
            x = _ffn_dense(x, tm, row(f_norm_g[layer]), bf(d_w_gate[i]), bf(d_w_up[i]), bf(d_w_down[i]))
        else:
            x = _moe_layer(x, tm, i, row(f_norm_g[layer]), bf(m_w_router[i].T), m_wg, m_wu, m_wd,
                           t_p if layer == depth - 1 else None)

    y_p, y_s = x if isinstance(x, (list, tuple)) else (x[:t_p], x[t_p:])
    return (y_p.reshape(bsz, seq, d), y_s.reshape(n_seq, dec, d),
            c[:t_p].reshape(bsz, seq, kv_rank), kr[:t_p].reshape(bsz, seq, D_ROPE),
            c[t_p:].reshape(n_seq, dec, kv_rank), kr[t_p:].reshape(n_seq, dec, D_ROPE),
            jnp.stack(v_rows))
```

```python
import functools

import jax
import jax.numpy as jnp
from jax import lax
from jax.experimental import pallas as pl
from jax.experimental.pallas import tpu as pltpu

EPS = 1e-6
ROPE_THETA = 10000.0
CHUNK = 128
N_GROUPS = 8
N_HEADS = 8
D_NOPE = 128
D_ROPE = 64
D_V = 128
TOP_K = 2
QBLOCK = 128
LANES = 128
VMEM_LIMIT_BYTES = 60 * 1024 * 1024
GMLP_COL_CHUNK = 512

BF16 = jnp.bfloat16
F32 = jnp.float32
NT_DIMS = (((1,), (1,)), ((), ()))


def _dot(a, b):
    return jnp.dot(a, b, preferred_element_type=F32)


def _dot_nt(a, b):
    return lax.dot_general(a, b, NT_DIMS, preferred_element_type=F32)


def _rms(x, g):
    ms = jnp.mean(x * x, axis=-1, keepdims=True)
    return x * lax.rsqrt(ms + EPS) * g


def _rope_pairs(a, g2, cs):
    ms = jnp.sum(a * a, axis=-1, keepdims=True) * (1.0 / LANES)
    t = a * lax.rsqrt(ms + EPS) * g2 * cs
    return t + pltpu.roll(t, D_ROPE, 1)


def _resident(shape):
    nd = len(shape)
    return pl.BlockSpec(shape, lambda *_: (0,) * nd, pipeline_mode=pl.Buffered(1))


def _params(sem):
    return pltpu.CompilerParams(dimension_semantics=sem, vmem_limit_bytes=VMEM_LIMIT_BYTES)


def _gmlp_kernel(n_p_tiles, n_x, *refs):
    x_refs = refs[:n_x]
    ng_ref, win_ref, bin_ref, sg_ref, ws_ref, bs_ref, wout_ref, bout_ref = refs[n_x:n_x + 8]
    xo_ref, v_ref, zu_ref, vb_ref, gated_ref = refs[n_x + 8:]
    i = pl.program_id(0)
    x = x_refs[0][...] if n_x == 1 else jnp.where(i < n_p_tiles, x_refs[0][...], x_refs[1][...])
    h = _rms(x, ng_ref[...]).astype(BF16)
    dg = sg_ref.shape[1]
    ssq = jnp.zeros((x.shape[0], 1), F32)
    for c0 in range(0, dg, GMLP_COL_CHUNK):
        cols = slice(c0, c0 + GMLP_COL_CHUNK)
        vcols = slice(dg + c0, dg + c0 + GMLP_COL_CHUNK)
        zv = jax.nn.gelu(_dot(h, win_ref[:, vcols]) + bin_ref[:, vcols])
        ssq = ssq + jnp.sum(zv * zv, axis=1, keepdims=True)
        v_ref[:, cols] = zv
        zu_ref[:, cols] = jax.nn.gelu(_dot(h, win_ref[:, cols]) + bin_ref[:, cols])
    v = v_ref[...] * lax.rsqrt(ssq * (1.0 / dg) + EPS) * sg_ref[...]
    vb_ref[...] = v.astype(BF16)
    v_ref[...] = v

    gw = dg // N_GROUPS
    for c in range(x.shape[0] // CHUNK):
        rows = slice(c * CHUNK, (c + 1) * CHUNK)
        for g in range(N_GROUPS):
            cols = slice(g * gw, (g + 1) * gw)
            mix = _dot(ws_ref[0, g], vb_ref[rows, cols]) + bs_ref[0, :, cols]
            gated_ref[rows, cols] = (zu_ref[rows, cols] * mix).astype(BF16)
    xo_ref[...] = x + _dot(gated_ref[...], wout_ref[...]) + bout_ref[...]


def _gmlp_layer(xs, n_p_tiles, tm, ng, w_in, b_in, sg, ws_eff, bs_eff, w_out, b_out):
    d = xs[0].shape[1]
    t = sum(a.shape[0] for a in xs)
    dg = sg.shape[1]
    n_tiles = t // tm
    t_s = t - n_p_tiles * tm
    sel = lambda i: jnp.where(i >= n_p_tiles, 1, 0)
    if len(xs) == 1:
        x_specs = [pl.BlockSpec((tm, d), lambda i: (i, 0))]
    else:
        x_specs = [pl.BlockSpec((tm, d), lambda i: (jnp.minimum(i, n_p_tiles - 1), 0)),
                   pl.BlockSpec((tm, d), lambda i: (jnp.maximum(i - n_p_tiles, 0), 0))]
    return pl.pallas_call(
        functools.partial(_gmlp_kernel, n_p_tiles, len(xs)),
        grid=(n_tiles,),
        in_specs=x_specs + [
            _resident(ng.shape), _resident(w_in.shape), _resident(b_in.shape), _resident(sg.shape),
            pl.BlockSpec((1,) + ws_eff.shape[1:], lambda i: (sel(i), 0, 0, 0)),
            pl.BlockSpec((1,) + bs_eff.shape[1:], lambda i: (sel(i), 0, 0)),
            _resident(w_out.shape), _resident(b_out.shape),
        ],
        out_specs=[
            pl.BlockSpec((tm, d), lambda i: (i, 0)),
            pl.BlockSpec((tm, dg), lambda i: (jnp.maximum(i - n_p_tiles, 0), 0)),
        ],
        out_shape=[jax.ShapeDtypeStruct((t, d), F32), jax.ShapeDtypeStruct((t_s, dg), F32)],
        scratch_shapes=[pltpu.VMEM((tm, dg), F32), pltpu.VMEM((tm, dg), BF16), pltpu.VMEM((tm, dg), BF16)],
        compiler_params=_params(("arbitrary",)),
        name="gmlp_layer",
    )(*xs, ng, w_in, b_in, sg, ws_eff, bs_eff, w_out, b_out)


def _ffn_kernel(x_ref, ng_ref, wg_ref, wu_ref, wd_ref, xo_ref):
    x = x_ref[...]
    h = _rms(x, ng_ref[...]).astype(BF16)
    t = (jax.nn.silu(_dot(h, wg_ref[...])) * _dot(h, wu_ref[...])).astype(BF16)
    xo_ref[...] = x + _dot(t, wd_ref[...])


def _ffn_dense(x, tm, ng, wg, wu, wd):
    t, d = x.shape
    return pl.pallas_call(
        _ffn_kernel,
        grid=(t // tm,),
        in_specs=[pl.BlockSpec((tm, d), lambda i: (i, 0)),
                  _resident(ng.shape), _resident(wg.shape), _resident(wu.shape), _resident(wd.shape)],
        out_specs=pl.BlockSpec((tm, d), lambda i: (i, 0)),
        out_shape=jax.ShapeDtypeStruct((t, d), F32),
        compiler_params=_params(("arbitrary",)),
        name="ffn_dense",
    )(x, ng, wg, wu, wd)


MOE_ROW_TILE = 512
DISPATCH_TOKENS = 1024
COMBINE_TOKENS = 256
SAMPLE_SUB_PAGES = 32


def _router_kernel(x_ref, ng_ref, wrt_ref, tri_ref, idx_ref, gate_ref, rank_ref, cnt_ref, run_ref):
    @pl.when(pl.program_id(0) == 0)
    def _():
        run_ref[...] = jnp.zeros_like(run_ref)

    hb = _rms(x_ref[...], ng_ref[...]).astype(BF16)
    logits = _dot_nt(wrt_ref[...], hb)
    ne = logits.shape[0]
    ex = jnp.exp(logits - jnp.max(logits, axis=0, keepdims=True))
    probs = ex / jnp.sum(ex, axis=0, keepdims=True)
    row = lax.broadcasted_iota(jnp.int32, probs.shape, 0)
    m1 = jnp.max(probs, axis=0, keepdims=True)
    i1 = jnp.min(jnp.where(probs == m1, row, ne), axis=0, keepdims=True)
    rest = jnp.where(row == i1, -1.0, probs)
    m2 = jnp.max(rest, axis=0, keepdims=True)
    i2 = jnp.min(jnp.where(rest == m2, row, ne), axis=0, keepdims=True)
    den = m1 + m2
    oh1 = row == i1
    oh2 = row == i2
    oh = jnp.where(oh1 | oh2, 1.0, 0.0)
    rank = _dot(oh.astype(BF16), tri_ref[...]) + run_ref[:, :1]
    idx_ref[...] = jnp.concatenate([i1, i2], axis=0)
    gate_ref[...] = jnp.concatenate([m1 / den, m2 / den], axis=0)
    rank_ref[...] = jnp.concatenate([jnp.sum(jnp.where(oh1, rank, 0.0), axis=0, keepdims=True),
                                     jnp.sum(jnp.where(oh2, rank, 0.0), axis=0, keepdims=True)],
                                    axis=0).astype(jnp.int32)
    run = run_ref[...] + jnp.sum(oh, axis=1, keepdims=True)
    run_ref[...] = run
    cnt_ref[...] = run.astype(jnp.int32)


def _router(x, tm, ng, wrt):
    t, d = x.shape
    ne = wrt.shape[0]
    tri = jnp.triu(jnp.ones((tm, tm), BF16), k=1)
    tok = lambda i: (0, i)
    return pl.pallas_call(
        _router_kernel,
        grid=(t // tm,),
        in_specs=[pl.BlockSpec((tm, d), lambda i: (i, 0)), _resident(ng.shape), _resident(wrt.shape),
                  _resident(tri.shape)],
        out_specs=[pl.BlockSpec((TOP_K, tm), tok), pl.BlockSpec((TOP_K, tm), tok), pl.BlockSpec((TOP_K, tm), tok),
                   pl.BlockSpec((ne, LANES), lambda i: (0, 0))],
        out_shape=[jax.ShapeDtypeStruct((TOP_K, t), jnp.int32), jax.ShapeDtypeStruct((TOP_K, t), F32),
                   jax.ShapeDtypeStruct((TOP_K, t), jnp.int32), jax.ShapeDtypeStruct((ne, LANES), jnp.int32)],
        scratch_shapes=[pltpu.VMEM((ne, LANES), F32)],
        compiler_params=_params(("arbitrary",)),
        name="moe_router",
    )(x, ng, wrt, tri)


def _row_copy(src_ref, src_row, dst_ref, dst_row, sem):
    return pltpu.make_async_copy(src_ref.at[pl.ds(src_row, 1)], dst_ref.at[pl.ds(dst_row, 1)], sem)


def _dispatch_kernel(fill_lo_ref, fill_hi_ref, pos_ref, x_ref, xs_hbm, zero_ref, sem, zsem):
    n = pos_ref.shape[1]

    @pl.when(pl.program_id(0) == 0)
    def _():
        zero_ref[...] = jnp.zeros_like(zero_ref)

        def fill(row, carry):
            _row_copy(zero_ref, 0, xs_hbm, row, zsem).start()
            return carry

        def drain(row, carry):
            _row_copy(zero_ref, 0, xs_hbm, row, zsem).wait()
            return carry

        for e in range(fill_lo_ref.shape[0]):
            lax.fori_loop(fill_lo_ref[e], fill_hi_ref[e], fill, 0)
        for e in range(fill_lo_ref.shape[0]):
            lax.fori_loop(fill_lo_ref[e], fill_hi_ref[e], drain, 0)

    def issue(r, carry):
        for k in range(TOP_K):
            _row_copy(x_ref, r, xs_hbm, pos_ref[k, r], sem).start()
        return carry

    lax.fori_loop(0, n, issue, 0, unroll=8)
    for k in range(TOP_K):
        pltpu.make_async_copy(x_ref, xs_hbm.at[pl.ds(0, n)], sem).wait()


def _dispatch(x, pos, fill_lo, fill_hi, n_rows):
    t, d = x.shape
    n = DISPATCH_TOKENS if t % DISPATCH_TOKENS == 0 else LANES
    return pl.pallas_call(
        _dispatch_kernel,
        grid_spec=pltpu.PrefetchScalarGridSpec(
            num_scalar_prefetch=2,
            grid=(t // n,),
            in_specs=[pl.BlockSpec((TOP_K, n), lambda i, lo, hi: (0, i), memory_space=pltpu.SMEM),
                      pl.BlockSpec((n, d), lambda i, lo, hi: (i, 0))],
            out_specs=pl.BlockSpec(memory_space=pl.ANY),
            scratch_shapes=[pltpu.VMEM((8, d), F32), pltpu.SemaphoreType.DMA(()), pltpu.SemaphoreType.DMA(())]),
        out_shape=jax.ShapeDtypeStruct((n_rows, d), F32),
        compiler_params=_params(("arbitrary",)),
        name="moe_dispatch",
    )(fill_lo, fill_hi, pos, x)


def _grouped_ffn_kernel(layer, te_ref, nu_ref, xs_ref, ng_ref, wg_ref, wu_ref, wd_ref, ys_ref):
    del layer, te_ref
    i = pl.program_id(0)

    @pl.when(i < nu_ref[0])
    def _():
        h = _rms(xs_ref[...], ng_ref[...]).astype(BF16)
        t = (jax.nn.silu(_dot(h, wg_ref[0, 0])) * _dot(h, wu_ref[0, 0])).astype(BF16)
        ys_ref[...] = _dot(t, wd_ref[0, 0])

    @pl.when(i >= nu_ref[0])
    def _():
        ys_ref[...] = jnp.zeros_like(ys_ref)


def _grouped_ffn(xs, tile_expert, n_used, layer, ng, wg, wu, wd):
    n_rows, d = xs.shape
    dff = wg.shape[3]
    tmg = MOE_ROW_TILE
    wspec = lambda shape: pl.BlockSpec((1, 1) + shape, lambda i, te, nu: (layer, te[i], 0, 0),
                                       pipeline_mode=pl.Buffered(1))
    return pl.pallas_call(
        functools.partial(_grouped_ffn_kernel, layer),
        grid_spec=pltpu.PrefetchScalarGridSpec(
            num_scalar_prefetch=2,
            grid=(n_rows // tmg,),
            in_specs=[pl.BlockSpec((tmg, d), lambda i, te, nu: (jnp.minimum(i, jnp.maximum(nu[0], 1) - 1), 0)),
                      pl.BlockSpec(ng.shape, lambda i, te, nu: (0, 0), pipeline_mode=pl.Buffered(1)),
                      wspec((d, dff)), wspec((d, dff)), wspec((dff, d))],
            out_specs=pl.BlockSpec((tmg, d), lambda i, te, nu: (i, 0))),
        out_shape=jax.ShapeDtypeStruct((n_rows, d), F32),
        compiler_params=_params(("arbitrary",)),
        name="moe_grouped_ffn",
    )(tile_expert, n_used, xs, ng, wg, wu, wd)


def _combine_kernel(n_first, pos_ref, pos_next_ref, x_ref, gate_ref, ys_hbm, *refs):
    outs, (buf_ref, sem) = refs[:-2], refs[-2:]
    n = x_ref.shape[0]
    i = pl.program_id(0)
    slot = i % 2

    def gather(p_ref, s):
        def issue(r, carry):
            for k in range(TOP_K):
                _row_copy(ys_hbm, p_ref[k, r], buf_ref.at[s, k], r, sem.at[s]).start()
            return carry
        lax.fori_loop(0, n, issue, 0, unroll=8)

    @pl.when(i == 0)
    def _():
        gather(pos_ref, 0)

    @pl.when(i + 1 < pl.num_programs(0))
    def _():
        gather(pos_next_ref, 1 - slot)

    for k in range(TOP_K):
        pltpu.make_async_copy(ys_hbm.at[pl.ds(0, n)], buf_ref.at[slot, k], sem.at[slot]).wait()
    g = gate_ref[...]
    y = x_ref[...] + (g[:, 0:1] * buf_ref[slot, 0] + g[:, 1:2] * buf_ref[slot, 1])
    if len(outs) == 1:
        outs[0][...] = y
    else:
        @pl.when(i < n_first)
        def _():
            outs[0][...] = y

        @pl.when(i >= n_first)
        def _():
            outs[1][...] = y


def _combine(x, pos, gates, ys, t_first=None):
    t, d = x.shape
    n = COMBINE_TOKENS if t % COMBINE_TOKENS == 0 else LANES
    n_steps = t // n
    if t_first is None:
        n_first = n_steps
        out_specs = pl.BlockSpec((n, d), lambda i: (i, 0))
        out_shape = jax.ShapeDtypeStruct((t, d), F32)
    else:
        assert t_first % n == 0 and 0 < t_first < t
        n_first = t_first // n
        out_specs = [pl.BlockSpec((n, d), lambda i: (jnp.minimum(i, n_first - 1), 0)),
                     pl.BlockSpec((n, d), lambda i: (jnp.maximum(i - n_first, 0), 0))]
        out_shape = [jax.ShapeDtypeStruct((t_first, d), F32), jax.ShapeDtypeStruct((t - t_first, d), F32)]
    return pl.pallas_call(
        functools.partial(_combine_kernel, n_first),
        grid=(n_steps,),
        in_specs=[pl.BlockSpec((TOP_K, n), lambda i: (0, i), memory_space=pltpu.SMEM),
                  pl.BlockSpec((TOP_K, n), lambda i: (0, jnp.minimum(i + 1, n_steps - 1)), memory_space=pltpu.SMEM),
                  pl.BlockSpec((n, d), lambda i: (i, 0)), pl.BlockSpec((n, TOP_K), lambda i: (i, 0)),
                  pl.BlockSpec(memory_space=pl.ANY)],
        out_specs=out_specs,
        out_shape=out_shape,
        scratch_shapes=[pltpu.VMEM((2, TOP_K, n, d), F32), pltpu.SemaphoreType.DMA((2,))],
        compiler_params=_params(("arbitrary",)),
        name="moe_combine",
    )(pos, pos, x, gates, ys)


def _moe_layer(x, tm, layer, ng, wrt, wg, wu, wd, t_first=None):
    t, d = x.shape
    ne = wrt.shape[0]
    tmg = MOE_ROW_TILE
    n_rows = -(-(t * TOP_K) // tmg) * tmg + ne * tmg
    idx, gate, rank, cnt = _router(x, tm, ng, wrt)
    padded = (cnt[:, 0] + tmg - 1) // tmg * tmg
    ends = jnp.cumsum(padded)
    starts = ends - padded
    pos = rank + sum(jnp.where(idx == e, starts[e], 0) for e in range(ne))
    tile_start = jnp.arange(n_rows // tmg, dtype=jnp.int32) * tmg
    tile_expert = jnp.minimum(jnp.sum(tile_start[:, None] >= ends[None, :], axis=1), ne - 1).astype(jnp.int32)
    n_used = (ends[-1:] // tmg).astype(jnp.int32)
    fill_hi = jnp.where(jnp.arange(ne) == ne - 1, n_rows, ends).astype(jnp.int32)
    xs = _dispatch(x, pos, starts + cnt[:, 0], fill_hi, n_rows)
    ys = _grouped_ffn(xs, tile_expert, n_used, layer, ng, wg, wu, wd)
    return _combine(x, pos, gate.T, ys, t_first)


def _keyside_kernel(x_ref, ng_ref, wdkv_ref, lg_ref, wkr_ref, krg_ref, cs_ref, wukt_ref,
                    c_ref, kr_ref, cb_ref, krp_ref, rt_ref):
    h = _rms(x_ref[...], ng_ref[...]).astype(BF16)
    c = _rms(_dot(h, wdkv_ref[...]), lg_ref[...])
    cb = c.astype(BF16)
    c_ref[...] = c
    cb_ref[...] = cb
    full = _rope_pairs(_dot(h, wkr_ref[...]), krg_ref[...], cs_ref[...])
    kr_ref[...] = full[:, :D_ROPE]
    lane = lax.broadcasted_iota(jnp.int32, full.shape, 1)
    krp_ref[...] = jnp.where(lane < D_ROPE, full, 0.0).astype(BF16)
    rt_ref[...] = _key_rms_scale_t(wukt_ref[...], cb)


def _key_rms_scale_t(wukt, cb):
    kt = _dot_nt(wukt, cb)
    ssq = jnp.sum((kt * kt).reshape(N_HEADS, D_NOPE, kt.shape[1]), axis=1)
    return lax.rsqrt(ssq * (1.0 / D_NOPE) + EPS)


def _key_side(x, tm, cs_map, ng, w_dkv, lg, w_kr2, krg2, cs_tab, wukt):
    t, d = x.shape
    r = w_dkv.shape[1]
    row = lambda i: (i, 0)
    return pl.pallas_call(
        _keyside_kernel,
        grid=(t // tm,),
        in_specs=[pl.BlockSpec((tm, d), row),
                  _resident(ng.shape), _resident(w_dkv.shape), _resident(lg.shape),
                  _resident(w_kr2.shape), _resident(krg2.shape),
                  pl.BlockSpec((tm, LANES), lambda i: (cs_map(i), 0)),
                  _resident(wukt.shape)],
        out_specs=[pl.BlockSpec((tm, r), row), pl.BlockSpec((tm, D_ROPE), row), pl.BlockSpec((tm, r), row),
                   pl.BlockSpec((tm, LANES), row), pl.BlockSpec((N_HEADS, tm), lambda i: (0, i))],
        out_shape=[jax.ShapeDtypeStruct((t, r), F32), jax.ShapeDtypeStruct((t, D_ROPE), F32),
                   jax.ShapeDtypeStruct((t, r), BF16), jax.ShapeDtypeStruct((t, LANES), BF16),
                   jax.ShapeDtypeStruct((N_HEADS, t), F32)],
        compiler_params=_params(("arbitrary",)),
        name="key_side",
    )(x, ng, w_dkv, lg, w_kr2, krg2, cs_tab, wukt)


def _paged_fetch(pt_ref, n_pages, n_pg, caches, bufs, sems):
    n_steps = pl.num_programs(1)
    last = pl.num_programs(0) * n_steps - 1
    t = pl.program_id(0) * n_steps + pl.program_id(1)
    slot = t % 2

    def start(step, s):
        base = (step // n_steps) * n_pages + (step % n_steps) * n_pg
        for g in range(n_pg):
            pid = pt_ref[base + g]
            for cache, buf, sem in zip(caches, bufs, sems):
                pltpu.make_async_copy(cache.at[pid], buf.at[s, g], sem.at[s]).start()

    def wait(s):
        for cache, buf, sem in zip(caches, bufs, sems):
            pltpu.make_async_copy(cache.at[pl.ds(0, n_pg)], buf.at[s], sem.at[s]).wait()

    @pl.when(t == 0)
    def _():
        start(t, 0)

    wait(slot)

    def prefetch():
        start(jnp.minimum(t + 1, last), 1 - slot)

    def drain():
        @pl.when(t == last)
        def _():
            wait(1 - slot)

    return slot, prefetch, drain


def _past_scale_kernel(n_pages, n_pg, pt_ref, wukt_ref, cache_hbm, rt_ref, cbuf_ref, sem):
    slot, prefetch, drain = _paged_fetch(pt_ref, n_pages, n_pg, [cache_hbm], [cbuf_ref], [sem])
    half = max(n_pg // 2, 1)
    for g0 in range(0, n_pg, half):
        cb = jnp.concatenate([cbuf_ref[slot, g].astype(BF16) for g in range(g0, g0 + half)], axis=0)
        rt_ref[0, :, g0 * CHUNK:(g0 + half) * CHUNK] = _key_rms_scale_t(wukt_ref[...], cb)
        if g0 == 0:
            prefetch()
    drain()


def _past_scale(page_table, cache_latent, wukt, n_pg):
    n_seq, n_pages = page_table.shape
    page, r = cache_latent.shape[1:]
    gk = n_pg * page
    return pl.pallas_call(
        functools.partial(_past_scale_kernel, n_pages, n_pg),
        grid_spec=pltpu.PrefetchScalarGridSpec(
            num_scalar_prefetch=1,
            grid=(n_seq, n_pages // n_pg),
            in_specs=[pl.BlockSpec(wukt.shape, lambda b, j, pt: (0, 0), pipeline_mode=pl.Buffered(1)),
                      pl.BlockSpec(memory_space=pl.ANY)],
            out_specs=pl.BlockSpec((1, N_HEADS, gk), lambda b, j, pt: (b, 0, j)),
            scratch_shapes=[pltpu.VMEM((2, n_pg, page, r), F32), pltpu.SemaphoreType.DMA((2,))]),
        out_shape=jax.ShapeDtypeStruct((n_seq, N_HEADS, n_pages * page), F32),
        compiler_params=_params(("arbitrary", "arbitrary")),
        name="past_key_scale",
    )(page_table.reshape(-1), wukt, cache_latent)


def _qside_kernel(x_ref, ng_ref, wdq_ref, qlg_ref, wn_ref, wr_ref, qng_ref, kng_ref, qrg_ref, cs_ref, wukt_ref,
                  qa_ref, qr_ref):
    h = _rms(x_ref[...], ng_ref[...]).astype(BF16)
    cq = _rms(_dot(h, wdq_ref[...]), qlg_ref[...]).astype(BF16)
    qn_all = _dot(cq, wn_ref[...])
    qr_all = _dot(cq, wr_ref[...])
    cs = cs_ref[...]
    for hh in range(N_HEADS):
        cols = slice(hh * LANES, (hh + 1) * LANES)
        qn = (_rms(qn_all[:, cols], qng_ref[...]) * kng_ref[...]).astype(BF16)
        qa_ref[hh] = _dot(qn, wukt_ref[hh]).astype(BF16)
        qr_ref[hh] = _rope_pairs(qr_all[:, cols], qrg_ref[...], cs).astype(BF16)


def _q_side(x, tm, cs_map, ng, w_dq, qlg, wn, wr2, qng, kng, qrg2, cs_tab, wukt3):
    t, d = x.shape
    r = wukt3.shape[2]
    return pl.pallas_call(
        _qside_kernel,
        grid=(t // tm,),
        in_specs=[pl.BlockSpec((tm, d), lambda i: (i, 0)),
                  _resident(ng.shape), _resident(w_dq.shape), _resident(qlg.shape), _resident(wn.shape),
                  _resident(wr2.shape), _resident(qng.shape), _resident(kng.shape), _resident(qrg2.shape),
                  pl.BlockSpec((tm, LANES), lambda i: (cs_map(i), 0)),
                  _resident(wukt3.shape)],
        out_specs=[pl.BlockSpec((N_HEADS, tm, r), lambda i: (0, i, 0)),
                   pl.BlockSpec((N_HEADS, tm, LANES), lambda i: (0, i, 0))],
        out_shape=[jax.ShapeDtypeStruct((N_HEADS, t, r), BF16), jax.ShapeDtypeStruct((N_HEADS, t, LANES), BF16)],
        compiler_params=_params(("arbitrary",)),
        name="q_side",
    )(x, ng, w_dq, qlg, wn, wr2, qng, kng, qrg2, cs_tab, wukt3)


def _softmax_step(s, cb, m, l, acc):
    m_new = jnp.maximum(m, jnp.max(s, axis=1, keepdims=True))
    alpha = jnp.exp(m - m_new)
    p = jnp.exp(s - m_new)
    l_new = alpha * l + jnp.sum(p, axis=1, keepdims=True)
    acc_new = alpha * acc + _dot(p.astype(BF16), cb)
    return m_new, l_new, acc_new


def _scores(s1, s2, rt, scale):
    nrow, nk = s1.shape
    s1 = s1.reshape(N_HEADS, nrow // N_HEADS, nk) * rt[:, None, :]
    return (s1.reshape(nrow, nk) + s2) * scale


def _prompt_attn_kernel(bq, bk, scale, qa_ref, qr_ref, cb_ref, krp_ref, rt_ref, o_ref, m_ref, acc_ref):
    i = pl.program_id(1)
    r = qa_ref.shape[2]
    q = qa_ref[...].reshape(N_HEADS * bq, r)
    qr = qr_ref[...].reshape(N_HEADS * bq, LANES)
    m_ref[...] = jnp.full_like(m_ref, -jnp.inf)
    acc_ref[...] = jnp.zeros_like(acc_ref)
    def step(k0, w, masked):
        cb = cb_ref[pl.ds(k0, w), :]
        rts = rt_ref[:, pl.ds(k0, w)] * scale
        s = _dot_nt(q, cb).reshape(N_HEADS, bq, w) * rts[:, None, :]
        s = s + (_dot_nt(qr, krp_ref[pl.ds(k0, w), :]) * scale).reshape(N_HEADS, bq, w)
        if masked:
            qpos = i * bq + lax.broadcasted_iota(jnp.int32, (1, bq, w), 1)
            kpos = k0 + lax.broadcasted_iota(jnp.int32, (1, bq, w), 2)
            s = jnp.where(kpos <= qpos, s, -jnp.inf)
        s = s.reshape(N_HEADS * bq, w)
        m_prev = m_ref[...]
        m_new = jnp.maximum(m_prev, jnp.max(s, axis=1, keepdims=True))
        alpha = jnp.exp(m_prev - m_new)
        p = jnp.exp(s - jnp.tile(m_new, (1, w // LANES))).astype(BF16)
        m_ref[...] = m_new
        acc_ref[...] = (jnp.tile(alpha, (1, acc_ref.shape[1] // LANES)) * acc_ref[...]
                        + _dot(p, jnp.concatenate([cb, jnp.ones((w, LANES), BF16)], axis=1)))

    def full_block(j, carry):
        step(pl.multiple_of(j * bk, bk), bk, False)
        return carry

    n_visible = (i * bq) // bk
    lax.fori_loop(0, n_visible, full_block, 0)
    k_diag = pl.multiple_of(n_visible * bk, bk)
    for part in range(bk // bq):
        @pl.when(i % (bk // bq) == part)
        def _():
            step(k_diag, (part + 1) * bq, True)
    denom = jnp.tile(acc_ref[:, r:], (1, r // LANES))
    o_ref[...] = (acc_ref[:, :r] / denom).reshape(N_HEADS, bq, r).astype(BF16)


def _prompt_attention(qa, qr, cb, krp, rt, bsz, seq, bq, bk, scale):
    r = cb.shape[1]
    nq = seq // bq
    nrow = N_HEADS * bq
    qmap = lambda b, i: (0, b * nq + i, 0)
    return pl.pallas_call(
        functools.partial(_prompt_attn_kernel, bq, bk, scale),
        grid=(bsz, nq),
        in_specs=[pl.BlockSpec((N_HEADS, bq, r), qmap), pl.BlockSpec((N_HEADS, bq, LANES), qmap),
                  pl.BlockSpec((seq, r), lambda b, i: (b, 0)), pl.BlockSpec((seq, LANES), lambda b, i: (b, 0)),
                  pl.BlockSpec((N_HEADS, seq), lambda b, i: (0, b))],
        out_specs=pl.BlockSpec((N_HEADS, bq, r), qmap),
        out_shape=jax.ShapeDtypeStruct((N_HEADS, bsz * seq, r), BF16),
        scratch_shapes=[pltpu.VMEM((nrow, LANES), F32), pltpu.VMEM((nrow, r + LANES), F32)],
        compiler_params=_params(("arbitrary", "arbitrary")),
        name="prompt_attention",
    )(qa, qr, cb, krp, rt)


def _sample_attn_kernel(n_pages, n_pg, n_new, scale, pt_ref, qa_ref, qr_ref, rtp_ref, cn_ref, krn_ref, rtn_ref,
                        cache_hbm, krope_hbm, o_ref, m_ref, l_ref, acc_ref, cbuf_ref, kbuf_ref, csem, ksem):
    j = pl.program_id(1)
    slot, prefetch, drain = _paged_fetch(pt_ref, n_pages, n_pg, [cache_hbm, krope_hbm], [cbuf_ref, kbuf_ref],
                                         [csem, ksem])

    @pl.when(j == 0)
    def _():
        m_ref[...] = jnp.full_like(m_ref, -jnp.inf)
        l_ref[...] = jnp.zeros_like(l_ref)
        acc_ref[...] = jnp.zeros_like(acc_ref)

    q = qa_ref[0]
    qr = qr_ref[0][:, :D_ROPE]
    sub = min(n_pg, SAMPLE_SUB_PAGES)
    parts = []
    for g0 in range(0, n_pg, sub):
        cb = jnp.concatenate([cbuf_ref[slot, g].astype(BF16) for g in range(g0, g0 + sub)], axis=0)
        kb = jnp.concatenate([kbuf_ref[slot, g].astype(BF16) for g in range(g0, g0 + sub)], axis=1)
        s = _scores(_dot_nt(q, cb), _dot(qr, kb), rtp_ref[0, :, g0 * CHUNK:(g0 + sub) * CHUNK], scale)
        mj = jnp.max(s, axis=1, keepdims=True)
        p = jnp.exp(s - mj)
        parts.append((mj, jnp.sum(p, axis=1, keepdims=True), _dot(p.astype(BF16), cb)))
        if g0 == 0:
            prefetch()
    m_prev = m_ref[...]
    m = functools.reduce(jnp.maximum, [mj for mj, _, _ in parts], m_prev)
    alpha = jnp.exp(m_prev - m)
    l = alpha * l_ref[...]
    acc = alpha * acc_ref[...]
    for mj, lj, accj in parts:
        aj = jnp.exp(mj - m)
        l = l + aj * lj
        acc = acc + aj * accj
    m_ref[...] = m
    l_ref[...] = l
    acc_ref[...] = acc

    @pl.when(j == pl.num_programs(1) - 1)
    def _():
        pad = CHUNK - n_new
        cn = jnp.concatenate([cn_ref[0], jnp.zeros((pad, cn_ref.shape[2]), F32)], axis=0).astype(BF16)
        krn = jnp.concatenate([krn_ref[0], jnp.zeros((pad, D_ROPE), F32)], axis=0).astype(BF16)
        sn = _scores(_dot_nt(q, cn), _dot_nt(qr, krn), rtn_ref[0], scale)
        nrow = q.shape[0]
        qi = lax.broadcasted_iota(jnp.int32, (N_HEADS, nrow // N_HEADS, CHUNK), 1).reshape(nrow, CHUNK)
        ki = lax.broadcasted_iota(jnp.int32, (nrow, CHUNK), 1)
        sn = jnp.where(ki <= qi, sn, -jnp.inf)
        _, l2, acc2 = _softmax_step(sn, cn, m, l, acc)
        o_ref[0] = (acc2 / l2).astype(BF16)

    drain()


def _sample_attention(page_table, qa_s, qr_s, rt_past, c_new, kr_new, rt_new, cache_latent, cache_krope_t,
                      n_pg, scale):
    n_seq, n_pages = page_table.shape
    page, r = cache_latent.shape[1:]
    nrow = qa_s.shape[1]
    n_new = c_new.shape[1]
    gk = n_pg * page
    seq_blk = lambda shape: pl.BlockSpec((1,) + shape, lambda b, j, pt: (b, 0, 0))
    any_spec = pl.BlockSpec(memory_space=pl.ANY)
    return pl.pallas_call(
        functools.partial(_sample_attn_kernel, n_pages, n_pg, n_new, scale),
        grid_spec=pltpu.PrefetchScalarGridSpec(
            num_scalar_prefetch=1,
            grid=(n_seq, n_pages // n_pg),
            in_specs=[seq_blk((nrow, r)), seq_blk((nrow, LANES)),
                      pl.BlockSpec((1, N_HEADS, gk), lambda b, j, pt: (b, 0, j)),
                      seq_blk((n_new, r)), seq_blk((n_new, D_ROPE)), seq_blk((N_HEADS, LANES)),
                      any_spec, any_spec],
            out_specs=seq_blk((nrow, r)),
            scratch_shapes=[pltpu.VMEM((nrow, 1), F32), pltpu.VMEM((nrow, 1), F32), pltpu.VMEM((nrow, r), F32),
                            pltpu.VMEM((2, n_pg, page, r), F32), pltpu.VMEM((2, n_pg, D_ROPE, page), F32),
                            pltpu.SemaphoreType.DMA((2,)), pltpu.SemaphoreType.DMA((2,))]),
        out_shape=jax.ShapeDtypeStruct((n_seq, nrow, r), BF16),
        compiler_params=_params(("arbitrary", "arbitrary")),
        name="sample_attention",
    )(page_table.reshape(-1), qa_s, qr_s, rt_past, c_new, kr_new, rt_new, cache_latent, cache_krope_t)


def _attn_out_kernel(n_p_tiles, x_ref, op_ref, os_ref, wuv_ref, wo_ref, xo_ref):
    i = pl.program_id(0)

    def project(o_ref):
        o = jnp.concatenate([_dot(o_ref[hh], wuv_ref[hh]) for hh in range(N_HEADS)], axis=1).astype(BF16)
        xo_ref[...] = x_ref[...] + _dot(o, wo_ref[...])

    @pl.when(i < n_p_tiles)
    def _():
        project(op_ref)

    @pl.when(i >= n_p_tiles)
    def _():
        project(os_ref)


def _attn_out(x, o_p, o_s, n_p_tiles, tm, w_uv3, w_o):
    t, d = x.shape
    r = o_p.shape[2]
    return pl.pallas_call(
        functools.partial(_attn_out_kernel, n_p_tiles),
        grid=(t // tm,),
        in_specs=[pl.BlockSpec((tm, d), lambda i: (i, 0)),
                  pl.BlockSpec((N_HEADS, tm, r), lambda i: (0, jnp.minimum(i, n_p_tiles - 1), 0)),
                  pl.BlockSpec((N_HEADS, tm, r), lambda i: (0, jnp.maximum(i - n_p_tiles, 0), 0)),
                  _resident(w_uv3.shape), _resident(w_o.shape)],
        out_specs=pl.BlockSpec((tm, d), lambda i: (i, 0)),
        out_shape=jax.ShapeDtypeStruct((t, d), F32),
        compiler_params=_params(("arbitrary",)),
        name="attn_out",
    )(x, o_p, o_s, w_uv3, w_o)


def _swap_halves(a):
    half = a.shape[-1] // 2
    return jnp.concatenate([a[..., half:], a[..., :half]], axis=-1)


def _rope_table(pos):
    half = D_ROPE // 2
    inv = ROPE_THETA ** (-jnp.arange(half, dtype=F32) / half)
    ang = pos.astype(F32)[:, None] * inv[None, :]
    cos, sin = jnp.cos(ang), jnp.sin(ang)
    return jnp.concatenate([cos, cos, -sin, sin], axis=1)


def _spatial_weights(w_s, b_s, length, gw):
    reps = CHUNK // length
    w = w_s[:, :length, :length] * jnp.tril(jnp.ones((length, length), w_s.dtype))
    w_eff = jnp.einsum("st,gpq->gsptq", jnp.eye(reps, dtype=w.dtype), w).reshape(N_GROUPS, CHUNK, CHUNK)
    b_eff = jnp.repeat(jnp.tile(b_s[:, :length].T, (reps, 1)), gw, axis=1)
    return w_eff, b_eff


def kernel(x_prompt, x_sample, cache_latent, cache_krope, page_table, a_norm_g, a_w_in, a_b_in, a_sgu_g, a_w_s, a_b_s, a_w_out, a_b_out, kv_norm_g, w_dkv, kv_lat_g, w_kr, kr_g, kn_g, w_uk, w_uv, b_norm_g, b_w_dq, b_q_lat_g, b_w_uq, b_qn_g, b_qr_g, b_w_o, f_norm_g, d_w_gate, d_w_up, d_w_down, m_w_router, m_w_gate, m_w_up, m_w_down):
    bsz, seq, d = x_prompt.shape
    n_seq, dec = x_sample.shape[:2]
    t_p, t_s = bsz * seq, n_seq * dec
    n_pages = page_table.shape[1]
    past_len = n_pages * cache_latent.shape[1]
    depth = f_norm_g.shape[0]
    n_a = a_w_in.shape[0]
    dg = a_sgu_g.shape[1]
    kv_rank = w_dkv.shape[1]
    scale = float((D_NOPE + D_ROPE) ** -0.5)

    tm = 512 if t_s % 512 == 0 else CHUNK
    assert t_p % tm == 0 and t_s % tm == 0 and seq % tm == 0 and tm % CHUNK == 0
    assert seq % CHUNK == 0 and CHUNK % dec == 0 and dec <= CHUNK and tm % dec == 0
    n_p_tiles = t_p // tm
    n_pg_attn = next(g for g in (64, 32, 16, 8, 4, 2, 1) if n_pages % g == 0)
    n_pg_scale = next(g for g in (32, 16, 8, 4, 2, 1) if n_pages % g == 0)
    kv_block = next(k for k in (512, 256, CHUNK) if seq % k == 0)
    row = lambda a: a.reshape(1, -1)
    bf = lambda a: a.astype(BF16)

    assert n_a >= 1
    x = None

    pos_p = jnp.arange(seq, dtype=jnp.int32)
    pos_s = past_len + jnp.arange(dec, dtype=jnp.int32)
    cs_tab = jnp.concatenate([_rope_table(pos_p), jnp.tile(_rope_table(pos_s), (tm // dec, 1))], axis=0)
    tiles_per_seq = seq // tm
    cs_map = lambda i: jnp.where(i < n_p_tiles, i % tiles_per_seq, tiles_per_seq)

    cache_krope_t = jnp.swapaxes(cache_krope, 1, 2)
    wukt3 = bf(jnp.transpose(w_uk, (1, 2, 0)))
    wukt = wukt3.reshape(N_HEADS * D_NOPE, kv_rank)
    w_uv3 = bf(jnp.transpose(w_uv, (1, 0, 2)))
    m_wg, m_wu, m_wd = bf(m_w_gate), bf(m_w_up), bf(m_w_down)

    v_rows = []
    c = kr = cb = krp = rt = rt_past = None
    for layer in range(depth):
        if layer < n_a:
            a = layer
            ws_p, bs_p = _spatial_weights(a_w_s[a], a_b_s[a], min(seq, CHUNK), dg // N_GROUPS)
            ws_s, bs_s = _spatial_weights(a_w_s[a], a_b_s[a], min(dec, CHUNK), dg // N_GROUPS)
            x_in = [x_prompt.reshape(t_p, d), x_sample.reshape(t_s, d)] if layer == 0 else [x]
            x, v = _gmlp_layer(x_in, n_p_tiles, tm, row(a_norm_g[a]), bf(a_w_in[a]), row(a_b_in[a]), row(a_sgu_g[a]),
                               bf(jnp.stack([ws_p, ws_s])), jnp.stack([bs_p, bs_s]), bf(a_w_out[a]), row(a_b_out[a]))
            v_rows.append(v.reshape(n_seq, dec, dg))
        else:
            if layer == n_a:
                w_kr2 = bf(jnp.concatenate([w_kr, _swap_halves(w_kr)], axis=1))
                krg2 = row(jnp.concatenate([kr_g, _swap_halves(kr_g)]))
                c, kr, cb, krp, rt = _key_side(x, tm, cs_map, row(kv_norm_g), bf(w_dkv), row(kv_lat_g), w_kr2, krg2,
                                               cs_tab, wukt)
                rt_past = _past_scale(page_table, cache_latent, wukt, n_pg_scale)
                c_new = c[t_p:].reshape(n_seq, dec, kv_rank)
                kr_new = kr[t_p:].reshape(n_seq, dec, D_ROPE)
                rt_new = jnp.transpose(rt[:, t_p:].reshape(N_HEADS, n_seq, dec), (1, 0, 2))
                rt_new = jnp.pad(rt_new, ((0, 0), (0, 0), (0, LANES - dec)))
            b = layer - n_a
            w_uq = b_w_uq[b]
            wn = bf(w_uq[:, :, :D_NOPE].reshape(w_uq.shape[0], N_HEADS * D_NOPE))
            wr = w_uq[:, :, D_NOPE:]
            wr2 = bf(jnp.concatenate([wr, _swap_halves(wr)], axis=2).reshape(w_uq.shape[0], N_HEADS * LANES))
            qrg2 = row(jnp.concatenate([b_qr_g[b], _swap_halves(b_qr_g[b])]))
            qa, qr = _q_side(x, tm, cs_map, row(b_norm_g[b]), bf(b_w_dq[b]), row(b_q_lat_g[b]), wn, wr2,
                             row(b_qn_g[b]), row(kn_g), qrg2, cs_tab, wukt3)
            o_p = _prompt_attention(qa, qr, cb, krp, rt, bsz, seq, QBLOCK, kv_block, scale)

            def per_seq(a):
                w = a.shape[2]
                return jnp.transpose(a[:, t_p:].reshape(N_HEADS, n_seq, dec, w), (1, 0, 2, 3)).reshape(
                    n_seq, N_HEADS * dec, w)

            o_s = _sample_attention(page_table, per_seq(qa), per_seq(qr), rt_past, c_new, kr_new, rt_new,
                                    cache_latent, cache_krope_t, n_pg_attn, scale)
            o_s = jnp.transpose(o_s.reshape(n_seq, N_HEADS, dec, kv_rank), (1, 0, 2, 3)).reshape(
                N_HEADS, t_s, kv_rank)
            x = _attn_out(x, o_p, o_s, n_p_tiles, tm, w_uv3, bf(b_w_o[b]))
        i = layer // 2
        if layer % 2 == 0:
            x = _ffn_dense(x, tm, row(f_norm_g[layer]), bf(d_w_gate[i]), bf(d_w_up[i]), bf(d_w_down[i]))
        else:
            x = _moe_layer(x, tm, i, row(f_norm_g[layer]), bf(m_w_router[i].T), m_wg, m_wu, m_wd,
                           t_p if layer == depth - 1 else None)

    y_p, y_s = x if isinstance(x, (list, tuple)) else (x[:t_p], x[t_p:])
    return (y_p.reshape(bsz, seq, d), y_s.reshape(n_seq, dec, d),
            c[:t_p].reshape(bsz, seq, kv_rank), kr[:t_p].reshape(bsz, seq, D_ROPE),
            c[t_p:].reshape(n_seq, dec, kv_rank), kr[t_p:].reshape(n_seq, dec, D_ROPE),
            jnp.stack(v_rows))
```

```python
import functools

import jax
import jax.numpy as jnp
from jax import lax
from jax.experimental import pallas as pl
from jax.experimental.pallas import tpu as pltpu

EPS = 1e-6
ROPE_THETA = 10000.0
CHUNK = 128
N_GROUPS = 8
N_HEADS = 8
D_NOPE = 128
D_ROPE = 64
D_V = 128
TOP_K = 2
QBLOCK = 128
LANES = 128
VMEM_LIMIT_BYTES = 60 * 1024 * 1024
GMLP_COL_CHUNK = 512

BF16 = jnp.bfloat16
F32 = jnp.float32
NT_DIMS = (((1,), (1,)), ((), ()))


def _dot(a, b):
    return jnp.dot(a, b, preferred_element_type=F32)


def _dot_nt(a, b):
    return lax.dot_general(a, b, NT_DIMS, preferred_element_type=F32)


def _rms(x, g):
    ms = jnp.mean(x * x, axis=-1, keepdims=True)
    return x * lax.rsqrt(ms + EPS) * g


def _rope_pairs(a, g2, cs):
    ms = jnp.sum(a * a, axis=-1, keepdims=True) * (1.0 / LANES)
    t = a * lax.rsqrt(ms + EPS) * g2 * cs
    return t + pltpu.roll(t, D_ROPE, 1)


def _resident(shape):
    nd = len(shape)
    return pl.BlockSpec(shape, lambda *_: (0,) * nd, pipeline_mode=pl.Buffered(1))


def _params(sem):
    return pltpu.CompilerParams(dimension_semantics=sem, vmem_limit_bytes=VMEM_LIMIT_BYTES)


def _gmlp_kernel(n_p_tiles, n_x, *refs):
    x_refs = refs[:n_x]
    ng_ref, win_ref, bin_ref, sg_ref, ws_ref, bs_ref, wout_ref, bout_ref = refs[n_x:n_x + 8]
    xo_ref, v_ref, zu_ref, vb_ref, gated_ref = refs[n_x + 8:]
    i = pl.program_id(0)
    x = x_refs[0][...] if n_x == 1 else jnp.where(i < n_p_tiles, x_refs[0][...], x_refs[1][...])
    h = _rms(x, ng_ref[...]).astype(BF16)
    dg = sg_ref.shape[1]
    ssq = jnp.zeros((x.shape[0], 1), F32)
    for c0 in range(0, dg, GMLP_COL_CHUNK):
        cols = slice(c0, c0 + GMLP_COL_CHUNK)
        vcols = slice(dg + c0, dg + c0 + GMLP_COL_CHUNK)
        zv = jax.nn.gelu(_dot(h, win_ref[:, vcols]) + bin_ref[:, vcols])
        ssq = ssq + jnp.sum(zv * zv, axis=1, keepdims=True)
        v_ref[:, cols] = zv
        zu_ref[:, cols] = jax.nn.gelu(_dot(h, win_ref[:, cols]) + bin_ref[:, cols])
    v = v_ref[...] * lax.rsqrt(ssq * (1.0 / dg) + EPS) * sg_ref[...]
    vb_ref[...] = v.astype(BF16)
    v_ref[...] = v

    gw = dg // N_GROUPS
    for c in range(x.shape[0] // CHUNK):
        rows = slice(c * CHUNK, (c + 1) * CHUNK)
        for g in range(N_GROUPS):
            cols = slice(g * gw, (g + 1) * gw)
            mix = _dot(ws_ref[0, g], vb_ref[rows, cols]) + bs_ref[0, :, cols]
            gated_ref[rows, cols] = (zu_ref[rows, cols] * mix).astype(BF16)
    xo_ref[...] = x + _dot(gated_ref[...], wout_ref[...]) + bout_ref[...]


def _gmlp_layer(xs, n_p_tiles, tm, ng, w_in, b_in, sg, ws_eff, bs_eff, w_out, b_out):
    d = xs[0].shape[1]
    t = sum(a.shape[0] for a in xs)
    dg = sg.shape[1]
    n_tiles = t // tm
    t_s = t - n_p_tiles * tm
    sel = lambda i: jnp.where(i >= n_p_tiles, 1, 0)
    if len(xs) == 1:
        x_specs = [pl.BlockSpec((tm, d), lambda i: (i, 0))]
    else:
        x_specs = [pl.BlockSpec((tm, d), lambda i: (jnp.minimum(i, n_p_tiles - 1), 0)),
                   pl.BlockSpec((tm, d), lambda i: (jnp.maximum(i - n_p_tiles, 0), 0))]
    return pl.pallas_call(
        functools.partial(_gmlp_kernel, n_p_tiles, len(xs)),
        grid=(n_tiles,),
        in_specs=x_specs + [
            _resident(ng.shape), _resident(w_in.shape), _resident(b_in.shape), _resident(sg.shape),
            pl.BlockSpec((1,) + ws_eff.shape[1:], lambda i: (sel(i), 0, 0, 0)),
            pl.BlockSpec((1,) + bs_eff.shape[1:], lambda i: (sel(i), 0, 0)),
            _resident(w_out.shape), _resident(b_out.shape),
        ],
        out_specs=[
            pl.BlockSpec((tm, d), lambda i: (i, 0)),
            pl.BlockSpec((tm, dg), lambda i: (jnp.maximum(i - n_p_tiles, 0), 0)),
        ],
        out_shape=[jax.ShapeDtypeStruct((t, d), F32), jax.ShapeDtypeStruct((t_s, dg), F32)],
        scratch_shapes=[pltpu.VMEM((tm, dg), F32), pltpu.VMEM((tm, dg), BF16), pltpu.VMEM((tm, dg), BF16)],
        compiler_params=_params(("arbitrary",)),
        name="gmlp_layer",
    )(*xs, ng, w_in, b_in, sg, ws_eff, bs_eff, w_out, b_out)


def _ffn_kernel(x_ref, ng_ref, wg_ref, wu_ref, wd_ref, xo_ref):
    x = x_ref[...]
    h = _rms(x, ng_ref[...]).astype(BF16)
    t = (jax.nn.silu(_dot(h, wg_ref[...])) * _dot(h, wu_ref[...])).astype(BF16)
    xo_ref[...] = x + _dot(t, wd_ref[...])


def _ffn_dense(x, tm, ng, wg, wu, wd):
    t, d = x.shape
    return pl.pallas_call(
        _ffn_kernel,
        grid=(t // tm,),
        in_specs=[pl.BlockSpec((tm, d), lambda i: (i, 0)),
                  _resident(ng.shape), _resident(wg.shape), _resident(wu.shape), _resident(wd.shape)],
        out_specs=pl.BlockSpec((tm, d), lambda i: (i, 0)),
        out_shape=jax.ShapeDtypeStruct((t, d), F32),
        compiler_params=_params(("arbitrary",)),
        name="ffn_dense",
    )(x, ng, wg, wu, wd)


MOE_ROW_TILE = 512
DISPATCH_TOKENS = 1024
COMBINE_TOKENS = 256
SAMPLE_SUB_PAGES = 8


def _router_kernel(x_ref, ng_ref, wrt_ref, tri_ref, idx_ref, gate_ref, rank_ref, cnt_ref, run_ref):
    @pl.when(pl.program_id(0) == 0)
    def _():
        run_ref[...] = jnp.zeros_like(run_ref)

    hb = _rms(x_ref[...], ng_ref[...]).astype(BF16)
    logits = _dot_nt(wrt_ref[...], hb)
    ne = logits.shape[0]
    ex = jnp.exp(logits - jnp.max(logits, axis=0, keepdims=True))
    probs = ex / jnp.sum(ex, axis=0, keepdims=True)
    row = lax.broadcasted_iota(jnp.int32, probs.shape, 0)
    m1 = jnp.max(probs, axis=0, keepdims=True)
    i1 = jnp.min(jnp.where(probs == m1, row, ne), axis=0, keepdims=True)
    rest = jnp.where(row == i1, -1.0, probs)
    m2 = jnp.max(rest, axis=0, keepdims=True)
    i2 = jnp.min(jnp.where(rest == m2, row, ne), axis=0, keepdims=True)
    den = m1 + m2
    oh1 = row == i1
    oh2 = row == i2
    oh = jnp.where(oh1 | oh2, 1.0, 0.0)
    rank = _dot(oh.astype(BF16), tri_ref[...]) + run_ref[:, :1]
    idx_ref[...] = jnp.concatenate([i1, i2], axis=0)
    gate_ref[...] = jnp.concatenate([m1 / den, m2 / den], axis=0)
    rank_ref[...] = jnp.concatenate([jnp.sum(jnp.where(oh1, rank, 0.0), axis=0, keepdims=True),
                                     jnp.sum(jnp.where(oh2, rank, 0.0), axis=0, keepdims=True)],
                                    axis=0).astype(jnp.int32)
    run = run_ref[...] + jnp.sum(oh, axis=1, keepdims=True)
    run_ref[...] = run
    cnt_ref[...] = run.astype(jnp.int32)


def _router(x, tm, ng, wrt):
    t, d = x.shape
    ne = wrt.shape[0]
    tri = jnp.triu(jnp.ones((tm, tm), BF16), k=1)
    tok = lambda i: (0, i)
    return pl.pallas_call(
        _router_kernel,
        grid=(t // tm,),
        in_specs=[pl.BlockSpec((tm, d), lambda i: (i, 0)), _resident(ng.shape), _resident(wrt.shape),
                  _resident(tri.shape)],
        out_specs=[pl.BlockSpec((TOP_K, tm), tok), pl.BlockSpec((TOP_K, tm), tok), pl.BlockSpec((TOP_K, tm), tok),
                   pl.BlockSpec((ne, LANES), lambda i: (0, 0))],
        out_shape=[jax.ShapeDtypeStruct((TOP_K, t), jnp.int32), jax.ShapeDtypeStruct((TOP_K, t), F32),
                   jax.ShapeDtypeStruct((TOP_K, t), jnp.int32), jax.ShapeDtypeStruct((ne, LANES), jnp.int32)],
        scratch_shapes=[pltpu.VMEM((ne, LANES), F32)],
        compiler_params=_params(("arbitrary",)),
        name="moe_router",
    )(x, ng, wrt, tri)


def _row_copy(src_ref, src_row, dst_ref, dst_row, sem):
    return pltpu.make_async_copy(src_ref.at[pl.ds(src_row, 1)], dst_ref.at[pl.ds(dst_row, 1)], sem)


def _dispatch_kernel(fill_lo_ref, fill_hi_ref, pos_ref, x_ref, xs_hbm, zero_ref, sem, zsem):
    n = pos_ref.shape[1]

    @pl.when(pl.program_id(0) == 0)
    def _():
        zero_ref[...] = jnp.zeros_like(zero_ref)

        def fill(row, carry):
            _row_copy(zero_ref, 0, xs_hbm, row, zsem).start()
            return carry

        def drain(row, carry):
            _row_copy(zero_ref, 0, xs_hbm, row, zsem).wait()
            return carry

        for e in range(fill_lo_ref.shape[0]):
            lax.fori_loop(fill_lo_ref[e], fill_hi_ref[e], fill, 0)
        for e in range(fill_lo_ref.shape[0]):
            lax.fori_loop(fill_lo_ref[e], fill_hi_ref[e], drain, 0)

    def issue(r, carry):
        for k in range(TOP_K):
            _row_copy(x_ref, r, xs_hbm, pos_ref[k, r], sem).start(priority=k % 2)
        return carry

    lax.fori_loop(0, n, issue, 0, unroll=8)
    for k in range(TOP_K):
        pltpu.make_async_copy(x_ref, xs_hbm.at[pl.ds(0, n)], sem).wait()


def _dispatch(x, pos, fill_lo, fill_hi, n_rows):
    t, d = x.shape
    n = DISPATCH_TOKENS if t % DISPATCH_TOKENS == 0 else LANES
    return pl.pallas_call(
        _dispatch_kernel,
        grid_spec=pltpu.PrefetchScalarGridSpec(
            num_scalar_prefetch=2,
            grid=(t // n,),
            in_specs=[pl.BlockSpec((TOP_K, n), lambda i, lo, hi: (0, i), memory_space=pltpu.SMEM),
                      pl.BlockSpec((n, d), lambda i, lo, hi: (i, 0))],
            out_specs=pl.BlockSpec(memory_space=pl.ANY),
            scratch_shapes=[pltpu.VMEM((8, d), F32), pltpu.SemaphoreType.DMA(()), pltpu.SemaphoreType.DMA(())]),
        out_shape=jax.ShapeDtypeStruct((n_rows, d), F32),
        compiler_params=_params(("arbitrary",)),
        name="moe_dispatch",
    )(fill_lo, fill_hi, pos, x)


def _grouped_ffn_kernel(layer, te_ref, nu_ref, xs_ref, ng_ref, wg_ref, wu_ref, wd_ref, ys_ref):
    del layer, te_ref
    i = pl.program_id(0)

    @pl.when(i < nu_ref[0])
    def _():
        h = _rms(xs_ref[...], ng_ref[...]).astype(BF16)
        t = (jax.nn.silu(_dot(h, wg_ref[0, 0])) * _dot(h, wu_ref[0, 0])).astype(BF16)
        ys_ref[...] = _dot(t, wd_ref[0, 0])

    @pl.when(i >= nu_ref[0])
    def _():
        ys_ref[...] = jnp.zeros_like(ys_ref)


def _grouped_ffn(xs, tile_expert, n_used, layer, ng, wg, wu, wd):
    n_rows, d = xs.shape
    dff = wg.shape[3]
    tmg = MOE_ROW_TILE
    wspec = lambda shape: pl.BlockSpec((1, 1) + shape, lambda i, te, nu: (layer, te[i], 0, 0),
                                       pipeline_mode=pl.Buffered(1))
    return pl.pallas_call(
        functools.partial(_grouped_ffn_kernel, layer),
        grid_spec=pltpu.PrefetchScalarGridSpec(
            num_scalar_prefetch=2,
            grid=(n_rows // tmg,),
            in_specs=[pl.BlockSpec((tmg, d), lambda i, te, nu: (jnp.minimum(i, jnp.maximum(nu[0], 1) - 1), 0)),
                      pl.BlockSpec(ng.shape, lambda i, te, nu: (0, 0), pipeline_mode=pl.Buffered(1)),
                      wspec((d, dff)), wspec((d, dff)), wspec((dff, d))],
            out_specs=pl.BlockSpec((tmg, d), lambda i, te, nu: (i, 0))),
        out_shape=jax.ShapeDtypeStruct((n_rows, d), F32),
        compiler_params=_params(("arbitrary",)),
        name="moe_grouped_ffn",
    )(tile_expert, n_used, xs, ng, wg, wu, wd)


def _combine_kernel(n_first, pos_ref, pos_next_ref, x_ref, gate_ref, ys_hbm, *refs):
    outs, (buf_ref, sem) = refs[:-2], refs[-2:]
    n = x_ref.shape[0]
    i = pl.program_id(0)
    slot = i % 2

    def gather(p_ref, s):
        def issue(r, carry):
            for k in range(TOP_K):
                _row_copy(ys_hbm, p_ref[k, r], buf_ref.at[s, k], r, sem.at[s]).start(priority=k % 2)
            return carry
        lax.fori_loop(0, n, issue, 0, unroll=8)

    @pl.when(i == 0)
    def _():
        gather(pos_ref, 0)

    @pl.when(i + 1 < pl.num_programs(0))
    def _():
        gather(pos_next_ref, 1 - slot)

    for k in range(TOP_K):
        pltpu.make_async_copy(ys_hbm.at[pl.ds(0, n)], buf_ref.at[slot, k], sem.at[slot]).wait()
    g = gate_ref[...]
    y = x_ref[...] + (g[:, 0:1] * buf_ref[slot, 0] + g[:, 1:2] * buf_ref[slot, 1])
    if len(outs) == 1:
        outs[0][...] = y
    else:
        @pl.when(i < n_first)
        def _():
            outs[0][...] = y

        @pl.when(i >= n_first)
        def _():
            outs[1][...] = y


def _combine(x, pos, gates, ys, t_first=None):
    t, d = x.shape
    n = COMBINE_TOKENS if t % COMBINE_TOKENS == 0 else LANES
    n_steps = t // n
    if t_first is None:
        n_first = n_steps
        out_specs = pl.BlockSpec((n, d), lambda i: (i, 0))
        out_shape = jax.ShapeDtypeStruct((t, d), F32)
    else:
        assert t_first % n == 0 and 0 < t_first < t
        n_first = t_first // n
        out_specs = [pl.BlockSpec((n, d), lambda i: (jnp.minimum(i, n_first - 1), 0)),
                     pl.BlockSpec((n, d), lambda i: (jnp.maximum(i - n_first, 0), 0))]
        out_shape = [jax.ShapeDtypeStruct((t_first, d), F32), jax.ShapeDtypeStruct((t - t_first, d), F32)]
    return pl.pallas_call(
        functools.partial(_combine_kernel, n_first),
        grid=(n_steps,),
        in_specs=[pl.BlockSpec((TOP_K, n), lambda i: (0, i), memory_space=pltpu.SMEM),
                  pl.BlockSpec((TOP_K, n), lambda i: (0, jnp.minimum(i + 1, n_steps - 1)), memory_space=pltpu.SMEM),
                  pl.BlockSpec((n, d), lambda i: (i, 0)), pl.BlockSpec((n, TOP_K), lambda i: (i, 0)),
                  pl.BlockSpec(memory_space=pl.ANY)],
        out_specs=out_specs,
        out_shape=out_shape,
        scratch_shapes=[pltpu.VMEM((2, TOP_K, n, d), F32), pltpu.SemaphoreType.DMA((2,))],
        compiler_params=_params(("arbitrary",)),
        name="moe_combine",
    )(pos, pos, x, gates, ys)


def _moe_layer(x, tm, layer, ng, wrt, wg, wu, wd, t_first=None):
    t, d = x.shape
    ne = wrt.shape[0]
    tmg = MOE_ROW_TILE
    n_rows = -(-(t * TOP_K) // tmg) * tmg + ne * tmg
    idx, gate, rank, cnt = _router(x, tm, ng, wrt)
    padded = (cnt[:, 0] + tmg - 1) // tmg * tmg
    ends = jnp.cumsum(padded)
    starts = ends - padded
    pos = rank + sum(jnp.where(idx == e, starts[e], 0) for e in range(ne))
    tile_start = jnp.arange(n_rows // tmg, dtype=jnp.int32) * tmg
    tile_expert = jnp.minimum(jnp.sum(tile_start[:, None] >= ends[None, :], axis=1), ne - 1).astype(jnp.int32)
    n_used = (ends[-1:] // tmg).astype(jnp.int32)
    fill_hi = jnp.where(jnp.arange(ne) == ne - 1, n_rows, ends).astype(jnp.int32)
    xs = _dispatch(x, pos, starts + cnt[:, 0], fill_hi, n_rows)
    ys = _grouped_ffn(xs, tile_expert, n_used, layer, ng, wg, wu, wd)
    return _combine(x, pos, gate.T, ys, t_first)


def _keyside_kernel(x_ref, ng_ref, wdkv_ref, lg_ref, wkr_ref, krg_ref, cs_ref, wukt_ref,
                    c_ref, kr_ref, cb_ref, krp_ref, rt_ref):
    h = _rms(x_ref[...], ng_ref[...]).astype(BF16)
    c = _rms(_dot(h, wdkv_ref[...]), lg_ref[...])
    cb = c.astype(BF16)
    c_ref[...] = c
    cb_ref[...] = cb
    full = _rope_pairs(_dot(h, wkr_ref[...]), krg_ref[...], cs_ref[...])
    kr_ref[...] = full[:, :D_ROPE]
    lane = lax.broadcasted_iota(jnp.int32, full.shape, 1)
    krp_ref[...] = jnp.where(lane < D_ROPE, full, 0.0).astype(BF16)
    rt_ref[...] = _key_rms_scale_t(wukt_ref[...], cb)


def _key_rms_scale_t(wukt, cb):
    kt = _dot_nt(wukt, cb)
    ssq = jnp.sum((kt * kt).reshape(N_HEADS, D_NOPE, kt.shape[1]), axis=1)
    return lax.rsqrt(ssq * (1.0 / D_NOPE) + EPS)


def _key_side(x, tm, cs_map, ng, w_dkv, lg, w_kr2, krg2, cs_tab, wukt):
    t, d = x.shape
    r = w_dkv.shape[1]
    row = lambda i: (i, 0)
    return pl.pallas_call(
        _keyside_kernel,
        grid=(t // tm,),
        in_specs=[pl.BlockSpec((tm, d), row),
                  _resident(ng.shape), _resident(w_dkv.shape), _resident(lg.shape),
                  _resident(w_kr2.shape), _resident(krg2.shape),
                  pl.BlockSpec((tm, LANES), lambda i: (cs_map(i), 0)),
                  _resident(wukt.shape)],
        out_specs=[pl.BlockSpec((tm, r), row), pl.BlockSpec((tm, D_ROPE), row), pl.BlockSpec((tm, r), row),
                   pl.BlockSpec((tm, LANES), row), pl.BlockSpec((N_HEADS, tm), lambda i: (0, i))],
        out_shape=[jax.ShapeDtypeStruct((t, r), F32), jax.ShapeDtypeStruct((t, D_ROPE), F32),
                   jax.ShapeDtypeStruct((t, r), BF16), jax.ShapeDtypeStruct((t, LANES), BF16),
                   jax.ShapeDtypeStruct((N_HEADS, t), F32)],
        compiler_params=_params(("arbitrary",)),
        name="key_side",
    )(x, ng, w_dkv, lg, w_kr2, krg2, cs_tab, wukt)


def _paged_fetch(pt_ref, n_pages, n_pg, caches, bufs, sems):
    n_steps = pl.num_programs(1)
    t = pl.program_id(0) * n_steps + pl.program_id(1)
    slot = t % 2

    def start(step, s):
        base = (step // n_steps) * n_pages + (step % n_steps) * n_pg
        for g in range(n_pg):
            pid = pt_ref[base + g]
            for cache, buf, sem in zip(caches, bufs, sems):
                pltpu.make_async_copy(cache.at[pid], buf.at[s, g], sem.at[s]).start()

    @pl.when(t == 0)
    def _():
        start(t, 0)

    @pl.when(t + 1 < pl.num_programs(0) * n_steps)
    def _():
        start(t + 1, 1 - slot)

    for cache, buf, sem in zip(caches, bufs, sems):
        pltpu.make_async_copy(cache.at[pl.ds(0, n_pg)], buf.at[slot], sem.at[slot]).wait()
    return slot


def _past_scale_kernel(n_pages, n_pg, pt_ref, wukt_ref, cache_hbm, rt_ref, cbuf_ref, sem):
    slot = _paged_fetch(pt_ref, n_pages, n_pg, [cache_hbm], [cbuf_ref], [sem])
    cb = jnp.concatenate([cbuf_ref[slot, g].astype(BF16) for g in range(n_pg)], axis=0)
    rt_ref[0] = _key_rms_scale_t(wukt_ref[...], cb)


def _past_scale(page_table, cache_latent, wukt, n_pg):
    n_seq, n_pages = page_table.shape
    page, r = cache_latent.shape[1:]
    gk = n_pg * page
    return pl.pallas_call(
        functools.partial(_past_scale_kernel, n_pages, n_pg),
        grid_spec=pltpu.PrefetchScalarGridSpec(
            num_scalar_prefetch=1,
            grid=(n_seq, n_pages // n_pg),
            in_specs=[pl.BlockSpec(wukt.shape, lambda b, j, pt: (0, 0), pipeline_mode=pl.Buffered(1)),
                      pl.BlockSpec(memory_space=pl.ANY)],
            out_specs=pl.BlockSpec((1, N_HEADS, gk), lambda b, j, pt: (b, 0, j)),
            scratch_shapes=[pltpu.VMEM((2, n_pg, page, r), F32), pltpu.SemaphoreType.DMA((2,))]),
        out_shape=jax.ShapeDtypeStruct((n_seq, N_HEADS, n_pages * page), F32),
        compiler_params=_params(("arbitrary", "arbitrary")),
        name="past_key_scale",
    )(page_table.reshape(-1), wukt, cache_latent)


def _qside_kernel(x_ref, ng_ref, wdq_ref, qlg_ref, wn_ref, wr_ref, qng_ref, kng_ref, qrg_ref, cs_ref, wukt_ref,
                  qa_ref, qr_ref):
    h = _rms(x_ref[...], ng_ref[...]).astype(BF16)
    cq = _rms(_dot(h, wdq_ref[...]), qlg_ref[...]).astype(BF16)
    qn_all = _dot(cq, wn_ref[...])
    qr_all = _dot(cq, wr_ref[...])
    cs = cs_ref[...]
    for hh in range(N_HEADS):
        cols = slice(hh * LANES, (hh + 1) * LANES)
        qn = (_rms(qn_all[:, cols], qng_ref[...]) * kng_ref[...]).astype(BF16)
        qa_ref[hh] = _dot(qn, wukt_ref[hh]).astype(BF16)
        qr_ref[hh] = _rope_pairs(qr_all[:, cols], qrg_ref[...], cs).astype(BF16)


def _q_side(x, tm, cs_map, ng, w_dq, qlg, wn, wr2, qng, kng, qrg2, cs_tab, wukt3):
    t, d = x.shape
    r = wukt3.shape[2]
    return pl.pallas_call(
        _qside_kernel,
        grid=(t // tm,),
        in_specs=[pl.BlockSpec((tm, d), lambda i: (i, 0)),
                  _resident(ng.shape), _resident(w_dq.shape), _resident(qlg.shape), _resident(wn.shape),
                  _resident(wr2.shape), _resident(qng.shape), _resident(kng.shape), _resident(qrg2.shape),
                  pl.BlockSpec((tm, LANES), lambda i: (cs_map(i), 0)),
                  _resident(wukt3.shape)],
        out_specs=[pl.BlockSpec((N_HEADS, tm, r), lambda i: (0, i, 0)),
                   pl.BlockSpec((N_HEADS, tm, LANES), lambda i: (0, i, 0))],
        out_shape=[jax.ShapeDtypeStruct((N_HEADS, t, r), BF16), jax.ShapeDtypeStruct((N_HEADS, t, LANES), BF16)],
        compiler_params=_params(("arbitrary",)),
        name="q_side",
    )(x, ng, w_dq, qlg, wn, wr2, qng, kng, qrg2, cs_tab, wukt3)


def _softmax_step(s, cb, m, l, acc):
    m_new = jnp.maximum(m, jnp.max(s, axis=1, keepdims=True))
    alpha = jnp.exp(m - m_new)
    p = jnp.exp(s - m_new)
    l_new = alpha * l + jnp.sum(p, axis=1, keepdims=True)
    acc_new = alpha * acc + _dot(p.astype(BF16), cb)
    return m_new, l_new, acc_new


def _scores(s1, s2, rt, scale):
    nrow, nk = s1.shape
    s1 = s1.reshape(N_HEADS, nrow // N_HEADS, nk) * rt[:, None, :]
    return (s1.reshape(nrow, nk) + s2) * scale


def _prompt_attn_kernel(bq, bk, scale, qa_ref, qr_ref, cb_ref, krp_ref, rt_ref, o_ref, m_ref, acc_ref):
    i = pl.program_id(1)
    r = qa_ref.shape[2]
    q = qa_ref[...].reshape(N_HEADS * bq, r)
    qr = qr_ref[...].reshape(N_HEADS * bq, LANES)
    m_ref[...] = jnp.full_like(m_ref, -jnp.inf)
    acc_ref[...] = jnp.zeros_like(acc_ref)
    def step(k0, w, masked):
        cb = cb_ref[pl.ds(k0, w), :]
        rts = rt_ref[:, pl.ds(k0, w)] * scale
        s = _dot_nt(q, cb).reshape(N_HEADS, bq, w) * rts[:, None, :]
        s = s + (_dot_nt(qr, krp_ref[pl.ds(k0, w), :]) * scale).reshape(N_HEADS, bq, w)
        if masked:
            qpos = i * bq + lax.broadcasted_iota(jnp.int32, (1, bq, w), 1)
            kpos = k0 + lax.broadcasted_iota(jnp.int32, (1, bq, w), 2)
            s = jnp.where(kpos <= qpos, s, -jnp.inf)
        s = s.reshape(N_HEADS * bq, w)
        m_prev = m_ref[...]
        m_new = jnp.maximum(m_prev, jnp.max(s, axis=1, keepdims=True))
        alpha = jnp.exp(m_prev - m_new)
        p = jnp.exp(s - jnp.tile(m_new, (1, w // LANES))).astype(BF16)
        m_ref[...] = m_new
        acc_ref[...] = (jnp.tile(alpha, (1, acc_ref.shape[1] // LANES)) * acc_ref[...]
                        + _dot(p, jnp.concatenate([cb, jnp.ones((w, LANES), BF16)], axis=1)))

    def full_block(j, carry):
        step(pl.multiple_of(j * bk, bk), bk, False)
        return carry

    n_visible = (i * bq) // bk
    lax.fori_loop(0, n_visible, full_block, 0)
    k_diag = pl.multiple_of(n_visible * bk, bk)
    for part in range(bk // bq):
        @pl.when(i % (bk // bq) == part)
        def _():
            step(k_diag, (part + 1) * bq, True)
    denom = jnp.tile(acc_ref[:, r:], (1, r // LANES))
    o_ref[...] = (acc_ref[:, :r] / denom).reshape(N_HEADS, bq, r).astype(BF16)


def _prompt_attention(qa, qr, cb, krp, rt, bsz, seq, bq, bk, scale):
    r = cb.shape[1]
    nq = seq // bq
    nrow = N_HEADS * bq
    qmap = lambda b, i: (0, b * nq + i, 0)
    return pl.pallas_call(
        functools.partial(_prompt_attn_kernel, bq, bk, scale),
        grid=(bsz, nq),
        in_specs=[pl.BlockSpec((N_HEADS, bq, r), qmap), pl.BlockSpec((N_HEADS, bq, LANES), qmap),
                  pl.BlockSpec((seq, r), lambda b, i: (b, 0)), pl.BlockSpec((seq, LANES), lambda b, i: (b, 0)),
                  pl.BlockSpec((N_HEADS, seq), lambda b, i: (0, b))],
        out_specs=pl.BlockSpec((N_HEADS, bq, r), qmap),
        out_shape=jax.ShapeDtypeStruct((N_HEADS, bsz * seq, r), BF16),
        scratch_shapes=[pltpu.VMEM((nrow, LANES), F32), pltpu.VMEM((nrow, r + LANES), F32)],
        compiler_params=_params(("arbitrary", "arbitrary")),
        name="prompt_attention",
    )(qa, qr, cb, krp, rt)


def _sample_attn_kernel(n_pages, n_pg, n_new, scale, pt_ref, qa_ref, qr_ref, rtp_ref, cn_ref, krn_ref, rtn_ref,
                        cache_hbm, krope_hbm, o_ref, m_ref, l_ref, acc_ref, cbuf_ref, kbuf_ref, csem, ksem):
    j = pl.program_id(1)
    slot = _paged_fetch(pt_ref, n_pages, n_pg, [cache_hbm, krope_hbm], [cbuf_ref, kbuf_ref], [csem, ksem])

    @pl.when(j == 0)
    def _():
        m_ref[...] = jnp.full_like(m_ref, -jnp.inf)
        l_ref[...] = jnp.zeros_like(l_ref)
        acc_ref[...] = jnp.zeros_like(acc_ref)

    q = qa_ref[0]
    qr = qr_ref[0][:, :D_ROPE]
    sub = min(n_pg, SAMPLE_SUB_PAGES)
    parts = []
    for g0 in range(0, n_pg, sub):
        cb = jnp.concatenate([cbuf_ref[slot, g].astype(BF16) for g in range(g0, g0 + sub)], axis=0)
        kb = jnp.concatenate([kbuf_ref[slot, g].astype(BF16) for g in range(g0, g0 + sub)], axis=1)
        s = _scores(_dot_nt(q, cb), _dot(qr, kb), rtp_ref[0, :, g0 * CHUNK:(g0 + sub) * CHUNK], scale)
        mj = jnp.max(s, axis=1, keepdims=True)
        p = jnp.exp(s - mj)
        parts.append((mj, jnp.sum(p, axis=1, keepdims=True), _dot(p.astype(BF16), cb)))
    m_prev = m_ref[...]
    m = functools.reduce(jnp.maximum, [mj for mj, _, _ in parts], m_prev)
    alpha = jnp.exp(m_prev - m)
    l = alpha * l_ref[...]
    acc = alpha * acc_ref[...]
    for mj, lj, accj in parts:
        aj = jnp.exp(mj - m)
        l = l + aj * lj
        acc = acc + aj * accj
    m_ref[...] = m
    l_ref[...] = l
    acc_ref[...] = acc

    @pl.when(j == pl.num_programs(1) - 1)
    def _():
        pad = CHUNK - n_new
        cn = jnp.concatenate([cn_ref[0], jnp.zeros((pad, cn_ref.shape[2]), F32)], axis=0).astype(BF16)
        krn = jnp.concatenate([krn_ref[0], jnp.zeros((pad, D_ROPE), F32)], axis=0).astype(BF16)
        sn = _scores(_dot_nt(q, cn), _dot_nt(qr, krn), rtn_ref[0], scale)
        nrow = q.shape[0]
        qi = lax.broadcasted_iota(jnp.int32, (N_HEADS, nrow // N_HEADS, CHUNK), 1).reshape(nrow, CHUNK)
        ki = lax.broadcasted_iota(jnp.int32, (nrow, CHUNK), 1)
        sn = jnp.where(ki <= qi, sn, -jnp.inf)
        _, l2, acc2 = _softmax_step(sn, cn, m, l, acc)
        o_ref[0] = (acc2 / l2).astype(BF16)


def _sample_attention(page_table, qa_s, qr_s, rt_past, c_new, kr_new, rt_new, cache_latent, cache_krope_t,
                      n_pg, scale):
    n_seq, n_pages = page_table.shape
    page, r = cache_latent.shape[1:]
    nrow = qa_s.shape[1]
    n_new = c_new.shape[1]
    gk = n_pg * page
    seq_blk = lambda shape: pl.BlockSpec((1,) + shape, lambda b, j, pt: (b, 0, 0))
    any_spec = pl.BlockSpec(memory_space=pl.ANY)
    return pl.pallas_call(
        functools.partial(_sample_attn_kernel, n_pages, n_pg, n_new, scale),
        grid_spec=pltpu.PrefetchScalarGridSpec(
            num_scalar_prefetch=1,
            grid=(n_seq, n_pages // n_pg),
            in_specs=[seq_blk((nrow, r)), seq_blk((nrow, LANES)),
                      pl.BlockSpec((1, N_HEADS, gk), lambda b, j, pt: (b, 0, j)),
                      seq_blk((n_new, r)), seq_blk((n_new, D_ROPE)), seq_blk((N_HEADS, LANES)),
                      any_spec, any_spec],
            out_specs=seq_blk((nrow, r)),
            scratch_shapes=[pltpu.VMEM((nrow, 1), F32), pltpu.VMEM((nrow, 1), F32), pltpu.VMEM((nrow, r), F32),
                            pltpu.VMEM((2, n_pg, page, r), F32), pltpu.VMEM((2, n_pg, D_ROPE, page), F32),
                            pltpu.SemaphoreType.DMA((2,)), pltpu.SemaphoreType.DMA((2,))]),
        out_shape=jax.ShapeDtypeStruct((n_seq, nrow, r), BF16),
        compiler_params=_params(("arbitrary", "arbitrary")),
        name="sample_attention",
    )(page_table.reshape(-1), qa_s, qr_s, rt_past, c_new, kr_new, rt_new, cache_latent, cache_krope_t)


def _attn_out_kernel(n_p_tiles, x_ref, op_ref, os_ref, wuv_ref, wo_ref, xo_ref):
    i = pl.program_id(0)

    def project(o_ref):
        o = jnp.concatenate([_dot(o_ref[hh], wuv_ref[hh]) for hh in range(N_HEADS)], axis=1).astype(BF16)
        xo_ref[...] = x_ref[...] + _dot(o, wo_ref[...])

    @pl.when(i < n_p_tiles)
    def _():
        project(op_ref)

    @pl.when(i >= n_p_tiles)
    def _():
        project(os_ref)


def _attn_out(x, o_p, o_s, n_p_tiles, tm, w_uv3, w_o):
    t, d = x.shape
    r = o_p.shape[2]
    return pl.pallas_call(
        functools.partial(_attn_out_kernel, n_p_tiles),
        grid=(t // tm,),
        in_specs=[pl.BlockSpec((tm, d), lambda i: (i, 0)),
                  pl.BlockSpec((N_HEADS, tm, r), lambda i: (0, jnp.minimum(i, n_p_tiles - 1), 0)),
                  pl.BlockSpec((N_HEADS, tm, r), lambda i: (0, jnp.maximum(i - n_p_tiles, 0), 0)),
                  _resident(w_uv3.shape), _resident(w_o.shape)],
        out_specs=pl.BlockSpec((tm, d), lambda i: (i, 0)),
        out_shape=jax.ShapeDtypeStruct((t, d), F32),
        compiler_params=_params(("arbitrary",)),
        name="attn_out",
    )(x, o_p, o_s, w_uv3, w_o)


def _swap_halves(a):
    half = a.shape[-1] // 2
    return jnp.concatenate([a[..., half:], a[..., :half]], axis=-1)


def _rope_table(pos):
    half = D_ROPE // 2
    inv = ROPE_THETA ** (-jnp.arange(half, dtype=F32) / half)
    ang = pos.astype(F32)[:, None] * inv[None, :]
    cos, sin = jnp.cos(ang), jnp.sin(ang)
    return jnp.concatenate([cos, cos, -sin, sin], axis=1)


def _spatial_weights(w_s, b_s, length, gw):
    reps = CHUNK // length
    w = w_s[:, :length, :length] * jnp.tril(jnp.ones((length, length), w_s.dtype))
    w_eff = jnp.einsum("st,gpq->gsptq", jnp.eye(reps, dtype=w.dtype), w).reshape(N_GROUPS, CHUNK, CHUNK)
    b_eff = jnp.repeat(jnp.tile(b_s[:, :length].T, (reps, 1)), gw, axis=1)
    return w_eff, b_eff


def kernel(x_prompt, x_sample, cache_latent, cache_krope, page_table, a_norm_g, a_w_in, a_b_in, a_sgu_g, a_w_s, a_b_s, a_w_out, a_b_out, kv_norm_g, w_dkv, kv_lat_g, w_kr, kr_g, kn_g, w_uk, w_uv, b_norm_g, b_w_dq, b_q_lat_g, b_w_uq, b_qn_g, b_qr_g, b_w_o, f_norm_g, d_w_gate, d_w_up, d_w_down, m_w_router, m_w_gate, m_w_up, m_w_down):
    bsz, seq, d = x_prompt.shape
    n_seq, dec = x_sample.shape[:2]
    t_p, t_s = bsz * seq, n_seq * dec
    n_pages = page_table.shape[1]
    past_len = n_pages * cache_latent.shape[1]
    depth = f_norm_g.shape[0]
    n_a = a_w_in.shape[0]
    dg = a_sgu_g.shape[1]
    kv_rank = w_dkv.shape[1]
    scale = float((D_NOPE + D_ROPE) ** -0.5)

    tm = 512 if t_s % 512 == 0 else CHUNK
    assert t_p % tm == 0 and t_s % tm == 0 and seq % tm == 0 and tm % CHUNK == 0
    assert seq % CHUNK == 0 and CHUNK % dec == 0 and dec <= CHUNK and tm % dec == 0
    n_p_tiles = t_p // tm
    n_pg_attn = next(g for g in (64, 32, 16, 8, 4, 2, 1) if n_pages % g == 0)
    n_pg_scale = next(g for g in (32, 16, 8, 4, 2, 1) if n_pages % g == 0)
    kv_block = next(k for k in (512, 256, CHUNK) if seq % k == 0)
    row = lambda a: a.reshape(1, -1)
    bf = lambda a: a.astype(BF16)

    assert n_a >= 1
    x = None

    pos_p = jnp.arange(seq, dtype=jnp.int32)
    pos_s = past_len + jnp.arange(dec, dtype=jnp.int32)
    cs_tab = jnp.concatenate([_rope_table(pos_p), jnp.tile(_rope_table(pos_s), (tm // dec, 1))], axis=0)
    tiles_per_seq = seq // tm
    cs_map = lambda i: jnp.where(i < n_p_tiles, i % tiles_per_seq, tiles_per_seq)

    cache_krope_t = jnp.swapaxes(cache_krope, 1, 2)
    wukt3 = bf(jnp.transpose(w_uk, (1, 2, 0)))
    wukt = wukt3.reshape(N_HEADS * D_NOPE, kv_rank)
    w_uv3 = bf(jnp.transpose(w_uv, (1, 0, 2)))
    m_wg, m_wu, m_wd = bf(m_w_gate), bf(m_w_up), bf(m_w_down)

    v_rows = []
    c = kr = cb = krp = rt = rt_past = None
    for layer in range(depth):
        if layer < n_a:
            a = layer
            ws_p, bs_p = _spatial_weights(a_w_s[a], a_b_s[a], min(seq, CHUNK), dg // N_GROUPS)
            ws_s, bs_s = _spatial_weights(a_w_s[a], a_b_s[a], min(dec, CHUNK), dg // N_GROUPS)
            x_in = [x_prompt.reshape(t_p, d), x_sample.reshape(t_s, d)] if layer == 0 else [x]
            x, v = _gmlp_layer(x_in, n_p_tiles, tm, row(a_norm_g[a]), bf(a_w_in[a]), row(a_b_in[a]), row(a_sgu_g[a]),
                               bf(jnp.stack([ws_p, ws_s])), jnp.stack([bs_p, bs_s]), bf(a_w_out[a]), row(a_b_out[a]))
            v_rows.append(v.reshape(n_seq, dec, dg))
        else:
            if layer == n_a:
                w_kr2 = bf(jnp.concatenate([w_kr, _swap_halves(w_kr)], axis=1))
                krg2 = row(jnp.concatenate([kr_g, _swap_halves(kr_g)]))
                c, kr, cb, krp, rt = _key_side(x, tm, cs_map, row(kv_norm_g), bf(w_dkv), row(kv_lat_g), w_kr2, krg2,
                                               cs_tab, wukt)
                rt_past = _past_scale(page_table, cache_latent, wukt, n_pg_scale)
                c_new = c[t_p:].reshape(n_seq, dec, kv_rank)
                kr_new = kr[t_p:].reshape(n_seq, dec, D_ROPE)
                rt_new = jnp.transpose(rt[:, t_p:].reshape(N_HEADS, n_seq, dec), (1, 0, 2))
                rt_new = jnp.pad(rt_new, ((0, 0), (0, 0), (0, LANES - dec)))
            b = layer - n_a
            w_uq = b_w_uq[b]
            wn = bf(w_uq[:, :, :D_NOPE].reshape(w_uq.shape[0], N_HEADS * D_NOPE))
            wr = w_uq[:, :, D_NOPE:]
            wr2 = bf(jnp.concatenate([wr, _swap_halves(wr)], axis=2).reshape(w_uq.shape[0], N_HEADS * LANES))
            qrg2 = row(jnp.concatenate([b_qr_g[b], _swap_halves(b_qr_g[b])]))
            qa, qr = _q_side(x, tm, cs_map, row(b_norm_g[b]), bf(b_w_dq[b]), row(b_q_lat_g[b]), wn, wr2,
                             row(b_qn_g[b]), row(kn_g), qrg2, cs_tab, wukt3)
            o_p = _prompt_attention(qa, qr, cb, krp, rt, bsz, seq, QBLOCK, kv_block, scale)

            def per_seq(a):
                w = a.shape[2]
                return jnp.transpose(a[:, t_p:].reshape(N_HEADS, n_seq, dec, w), (1, 0, 2, 3)).reshape(
                    n_seq, N_HEADS * dec, w)

            o_s = _sample_attention(page_table, per_seq(qa), per_seq(qr), rt_past, c_new, kr_new, rt_new,
                                    cache_latent, cache_krope_t, n_pg_attn, scale)
            o_s = jnp.transpose(o_s.reshape(n_seq, N_HEADS, dec, kv_rank), (1, 0, 2, 3)).reshape(
                N_HEADS, t_s, kv_rank)
            x = _attn_out(x, o_p, o_s, n_p_tiles, tm, w_uv3, bf(b_w_o[b]))
        i = layer // 2
        if layer % 2 == 0:
            x = _ffn_dense(x, tm, row(f_norm_g[layer]), bf(d_w_gate[i]), bf(d_w_up[i]), bf(d_w_down[i]))
        else:
            x = _moe_layer(x, tm, i, row(f_norm_g[layer]), bf(m_w_router[i].T), m_wg, m_wu, m_wd,
                           t_p if layer == depth - 1 else None)

    y_p, y_s = x if isinstance(x, (list, tuple)) else (x[:t_p], x[t_p:])
    return (y_p.reshape(bsz, seq, d), y_s.reshape(n_seq, dec, d),
            c[:t_p].reshape(bsz, seq, kv_rank), kr[:t_p].reshape(bsz, seq, D_ROPE),
            c[t_p:].reshape(n_seq, dec, kv_rank), kr[t_p:].reshape(n_seq, dec, D_ROPE),
            jnp.stack(v_rows))
```
